```python
import jax
import jax.numpy as jnp
from jax import lax
import numpy as np

D_MODEL = 2048
BATCH = 4
SEQ = 2048
DEPTH = 2
DEC_BATCH = 128
DEC_SEQ = 8
PAST_LEN = 16384
PAGE_SIZE = 128

N_MIXERS = 2
N_A_LAYERS = (DEPTH + 1) // 2
N_B_LAYERS = DEPTH // 2
EPS = 1e-6

GDN_HEAD_K = 128
GDN_HEAD_V = 128
GDN_QK_HEADS = D_MODEL // GDN_HEAD_K
GDN_V_HEADS = 2 * GDN_QK_HEADS
GDN_KEY_DIM = GDN_QK_HEADS * GDN_HEAD_K
GDN_VAL_DIM = GDN_V_HEADS * GDN_HEAD_V
GDN_CONV_DIM = 2 * GDN_KEY_DIM + GDN_VAL_DIM
GDN_PROJ_DIM = GDN_CONV_DIM + GDN_VAL_DIM + 2 * GDN_V_HEADS
GDN_CONV = 4
GDN_CHUNK = 64

CONF_CH = D_MODEL
CONF_K = 31

PEER_HEADS = 8
PEER_NKEYS = 128
PEER_EXPERTS = PEER_NKEYS * PEER_NKEYS
PEER_QDIM = 256
PEER_HALF = PEER_QDIM // 2
PEER_TOPK = 16
PEER_TOKEN_BLOCK = 128

PLE_DIM = 256

kernel_name = 'hybrid_gdn_conformer_peer_step'


def rms_norm(x, g):
    xf = x.astype(jnp.float32)
    y = xf * lax.rsqrt(jnp.mean(xf * xf, axis=-1, keepdims=True) + EPS)
    return (y * g.astype(jnp.float32)).astype(x.dtype)


def layer_norm(x, g, b):
    xf = x.astype(jnp.float32)
    mu = jnp.mean(xf, axis=-1, keepdims=True)
    xc = xf - mu
    y = xc * lax.rsqrt(jnp.mean(xc * xc, axis=-1, keepdims=True) + EPS)
    return (y * g.astype(jnp.float32) + b.astype(jnp.float32)).astype(x.dtype)


def l2_normalize(x):
    return x * lax.rsqrt(jnp.sum(x * x, axis=-1, keepdims=True) + EPS)


def causal_depthwise_conv(x, buf, w):
    K, C = w.shape
    xx = jnp.concatenate([buf.astype(x.dtype), x], axis=1)
    y = lax.conv_general_dilated(xx, w[:, None, :].astype(x.dtype), window_strides=(1,),
                                 padding='VALID', dimension_numbers=('NWC', 'WIO', 'NWC'),
                                 feature_group_count=C)
    return y, xx[:, xx.shape[1] - (K - 1):]


def gated_delta_rule(q, k, v, g, beta, S0):
    B, T, H, DK = q.shape
    DV = v.shape[-1]
    C = min(GDN_CHUNK, T)
    pad = (-T) % C
    if pad:
        pw = ((0, 0), (0, pad), (0, 0), (0, 0))
        q, k, v = jnp.pad(q, pw), jnp.pad(k, pw), jnp.pad(v, pw)
        g, beta = jnp.pad(g, pw[:3]), jnp.pad(beta, pw[:3])
    N = (T + pad) // C

    def to_blocks(t):
        return t.reshape(B, N, C, H, -1).transpose(1, 0, 3, 2, 4)

    q, k, v = to_blocks(q), to_blocks(k), to_blocks(v)
    g = g.reshape(B, N, C, H).transpose(1, 0, 3, 2)
    beta = beta.reshape(B, N, C, H).transpose(1, 0, 3, 2)
    gc = jnp.cumsum(g, axis=-1)
    kb = k * beta[..., None]
    vb = v * beta[..., None]
    tri = jnp.tril(jnp.ones((C, C), dtype=bool))
    strict = jnp.tril(jnp.ones((C, C), dtype=bool), k=-1)
    decay = jnp.exp(jnp.where(tri, gc[..., :, None] - gc[..., None, :], -jnp.inf))
    L = jnp.where(strict, jnp.einsum('nbhik,nbhjk->nbhij', kb, k) * decay, 0.0)
    eye = jnp.eye(C, dtype=jnp.float32)
    Tm = lax.linalg.triangular_solve(L + eye, jnp.broadcast_to(eye, L.shape),
                                     left_side=True, lower=True)
    u = jnp.einsum('nbhij,nbhjd->nbhid', Tm, vb)
    w = jnp.einsum('nbhij,nbhjd->nbhid', Tm, kb * jnp.exp(gc)[..., None])

    def step(S, xs):
        qc, kc, uc, wc, gcc, dec = xs
        v_new = uc - jnp.einsum('bhck,bhkv->bhcv', wc, S)
        attn = jnp.einsum('bhik,bhjk->bhij', qc, kc) * dec
        o = (jnp.einsum('bhck,bhkv->bhcv', qc * jnp.exp(gcc)[..., None], S)
             + jnp.einsum('bhij,bhjv->bhiv', attn, v_new))
        glast = gcc[..., -1]
        S = (S * jnp.exp(glast)[..., None, None]
             + jnp.einsum('bhck,bhcv->bhkv', kc * jnp.exp(glast[..., None] - gcc)[..., None], v_new))
        return S, o

    S, o = lax.scan(step, S0, (q, k, u, w, gc, decay))
    o = o.transpose(1, 0, 3, 2, 4).reshape(B, N * C, H, DV)[:, :T]
    return o, S


def gated_deltanet(a, conv_buf, S0, w_in, conv_w, a_log, dt_bias, o_norm, w_out):
    B, T, _ = a.shape
    proj = a @ w_in
    qkv = proj[..., :GDN_CONV_DIM]
    z = proj[..., GDN_CONV_DIM:GDN_CONV_DIM + GDN_VAL_DIM]
    b_raw = proj[..., GDN_CONV_DIM + GDN_VAL_DIM:GDN_CONV_DIM + GDN_VAL_DIM + GDN_V_HEADS]
    a_raw = proj[..., GDN_CONV_DIM + GDN_VAL_DIM + GDN_V_HEADS:]
    qkv_c, new_buf = causal_depthwise_conv(qkv, conv_buf, conv_w)
    qkv_c = jax.nn.silu(qkv_c.astype(jnp.float32))
    q = qkv_c[..., :GDN_KEY_DIM].reshape(B, T, GDN_QK_HEADS, GDN_HEAD_K)
    k = qkv_c[..., GDN_KEY_DIM:2 * GDN_KEY_DIM].reshape(B, T, GDN_QK_HEADS, GDN_HEAD_K)
    v = qkv_c[..., 2 * GDN_KEY_DIM:].reshape(B, T, GDN_V_HEADS, GDN_HEAD_V)
    rep = GDN_V_HEADS // GDN_QK_HEADS
    q = jnp.repeat(l2_normalize(q) * (GDN_HEAD_K ** -0.5), rep, axis=2)
    k = jnp.repeat(l2_normalize(k), rep, axis=2)
    beta = jax.nn.sigmoid(b_raw.astype(jnp.float32))
    g = -jnp.exp(a_log.astype(jnp.float32)) * jax.nn.softplus(
        a_raw.astype(jnp.float32) + dt_bias.astype(jnp.float32))
    o, S = gated_delta_rule(q, k, v, g, beta, S0.astype(jnp.float32))
    zf = z.astype(jnp.float32).reshape(B, T, GDN_V_HEADS, GDN_HEAD_V)
    o = rms_norm(o, o_norm) * jax.nn.silu(zf)
    y = o.reshape(B, T, GDN_VAL_DIM).astype(a.dtype) @ w_out
    return y, new_buf, S.astype(a.dtype)


def conformer_conv(a, buf, w_in, b_in, dw_w, dw_b, ln_g, ln_b, w_out, b_out):
    h = a @ w_in + b_in
    glu = h[..., :CONF_CH] * jax.nn.sigmoid(h[..., CONF_CH:])
    c, new_buf = causal_depthwise_conv(glu, buf, dw_w)
    c = jax.nn.silu(layer_norm(c + dw_b, ln_g, ln_b))
    return c @ w_out + b_out, new_buf


def peer_ffn(xn, w_q, keys1, keys2, u, v):
    B, T, D = xn.shape
    n = B * T
    xt = xn.reshape(n, D)
    q = (xt @ w_q).reshape(n, PEER_HEADS, PEER_QDIM).astype(jnp.float32)
    s1 = jnp.einsum('nhd,kd->nhk', q[..., :PEER_HALF], keys1.astype(jnp.float32))
    s2 = jnp.einsum('nhd,kd->nhk', q[..., PEER_HALF:], keys2.astype(jnp.float32))
    v1, i1 = lax.top_k(s1, PEER_TOPK)
    v2, i2 = lax.top_k(s2, PEER_TOPK)
    cand = (v1[..., :, None] + v2[..., None, :]).reshape(n, PEER_HEADS, PEER_TOPK * PEER_TOPK)
    cand_id = (i1[..., :, None] * PEER_NKEYS + i2[..., None, :]).reshape(n, PEER_HEADS, PEER_TOPK * PEER_TOPK)
    sc, pos = lax.top_k(cand, PEER_TOPK)
    eid = jnp.take_along_axis(cand_id, pos, axis=-1)
    gate = jax.nn.softmax(sc, axis=-1)
    nb = -(-n // PEER_TOKEN_BLOCK)
    pad = nb * PEER_TOKEN_BLOCK - n
    xt_p = jnp.pad(xt, ((0, pad), (0, 0))).reshape(nb, PEER_TOKEN_BLOCK, D)
    eid_p = jnp.pad(eid, ((0, pad), (0, 0), (0, 0))).reshape(nb, PEER_TOKEN_BLOCK, PEER_HEADS, PEER_TOPK)
    gate_p = jnp.pad(gate, ((0, pad), (0, 0), (0, 0))).reshape(nb, PEER_TOKEN_BLOCK, PEER_HEADS, PEER_TOPK)

    def block(args):
        xb, eb, gb = args
        hb = jnp.einsum('thkd,td->thk', u[eb], xb).astype(jnp.float32)
        ab = (gb * jax.nn.gelu(hb, approximate=False)).astype(xb.dtype)
        return jnp.einsum('thk,thkd->td', ab, v[eb])

    out = lax.map(block, (xt_p, eid_p, gate_p))
    return out.reshape(nb * PEER_TOKEN_BLOCK, D)[:n].reshape(B, T, D)


def decoder_trunk(x, p, gdn_S, gdn_buf, conf_buf, norm_mix, norm_ffn, norm_ple, final_norm,
                  gdn_w_in, gdn_conv_w, gdn_a_log, gdn_dt_bias, gdn_o_norm, gdn_w_out,
                  conf_w_in, conf_b_in, conf_dw_w, conf_dw_b, conf_ln_g, conf_ln_b, conf_w_out, conf_b_out,
                  peer_w_q, peer_keys1, peer_keys2, peer_u, peer_v, ple_w_proj, ple_w_gate, ple_b_gate):
    h = x
    new_S, new_gbuf, new_cbuf = [], [], []
    for i in range(DEPTH):
        j = i // N_MIXERS
        a = rms_norm(h, norm_mix[i])
        if i % N_MIXERS == 0:
            y, buf, S = gated_deltanet(a, gdn_buf[j], gdn_S[j], gdn_w_in[j], gdn_conv_w[j], gdn_a_log[j],
                                       gdn_dt_bias[j], gdn_o_norm[j], gdn_w_out[j])
            new_S.append(S)
            new_gbuf.append(buf)
        else:
            y, buf = conformer_conv(a, conf_buf[j], conf_w_in[j], conf_b_in[j], conf_dw_w[j], conf_dw_b[j],
                                    conf_ln_g[j], conf_ln_b[j], conf_w_out[j], conf_b_out[j])
            new_cbuf.append(buf)
        h = h + y
        h = h + peer_ffn(rms_norm(h, norm_ffn[i]), peer_w_q[i], peer_keys1[i], peer_keys2[i],
                         peer_u[i], peer_v[i])
        gate = jax.nn.sigmoid(rms_norm(h, norm_ple[i]) @ ple_w_gate[i] + ple_b_gate[i])
        h = h + (p[i].astype(h.dtype) @ ple_w_proj[i]) * gate
    return rms_norm(h, final_norm), jnp.stack(new_S), jnp.stack(new_gbuf), jnp.stack(new_cbuf)


def setup_inputs(seed: int = 0) -> dict:
    key = jax.random.key(seed)
    ks = iter(jax.random.split(key, 64))
    f32 = jnp.float32

    def nrm(shape, scale):
        return jax.random.normal(next(ks), shape, f32) * scale

    def gain(shape):
        return 1.0 + 0.05 * jax.random.normal(next(ks), shape, f32)

    d = {}
    d['x_prompt'] = nrm((BATCH, SEQ, D_MODEL), 1.0)
    d['x_sample'] = nrm((DEC_BATCH, DEC_SEQ, D_MODEL), 1.0)
    d['p_prompt'] = nrm((DEPTH, BATCH, SEQ, PLE_DIM), 1.0)
    d['p_sample'] = nrm((DEPTH, DEC_BATCH, DEC_SEQ, PLE_DIM), 1.0)
    d['state_gdn_recurrent'] = nrm((N_A_LAYERS, DEC_BATCH, GDN_V_HEADS, GDN_HEAD_K, GDN_HEAD_V), 0.1)
    d['state_gdn_conv'] = nrm((N_A_LAYERS, DEC_BATCH, GDN_CONV - 1, GDN_CONV_DIM), 1.0)
    d['state_conf_conv'] = nrm((N_B_LAYERS, DEC_BATCH, CONF_K - 1, CONF_CH), 0.5)
    d['norm_mix'] = gain((DEPTH, D_MODEL))
    d['norm_ffn'] = gain((DEPTH, D_MODEL))
    d['norm_ple'] = gain((DEPTH, D_MODEL))
    d['final_norm'] = gain((D_MODEL,))
    d['gdn_w_in'] = nrm((N_A_LAYERS, D_MODEL, GDN_PROJ_DIM), D_MODEL ** -0.5)
    d['gdn_conv_w'] = nrm((N_A_LAYERS, GDN_CONV, GDN_CONV_DIM), GDN_CONV ** -0.5)
    d['gdn_a_log'] = jnp.log(jax.random.uniform(next(ks), (N_A_LAYERS, GDN_V_HEADS), f32, 1.0, 16.0))
    d['gdn_dt_bias'] = nrm((N_A_LAYERS, GDN_V_HEADS), 0.1)
    d['gdn_o_norm'] = gain((N_A_LAYERS, GDN_HEAD_V))
    d['gdn_w_out'] = nrm((N_A_LAYERS, GDN_VAL_DIM, D_MODEL), GDN_VAL_DIM ** -0.5)
    d['conf_w_in'] = nrm((N_B_LAYERS, D_MODEL, 2 * CONF_CH), D_MODEL ** -0.5)
    d['conf_b_in'] = nrm((N_B_LAYERS, 2 * CONF_CH), 0.02)
    d['conf_dw_w'] = nrm((N_B_LAYERS, CONF_K, CONF_CH), CONF_K ** -0.5)
    d['conf_dw_b'] = nrm((N_B_LAYERS, CONF_CH), 0.02)
    d['conf_ln_g'] = gain((N_B_LAYERS, CONF_CH))
    d['conf_ln_b'] = nrm((N_B_LAYERS, CONF_CH), 0.02)
    d['conf_w_out'] = nrm((N_B_LAYERS, CONF_CH, D_MODEL), CONF_CH ** -0.5)
    d['conf_b_out'] = nrm((N_B_LAYERS, D_MODEL), 0.02)
    d['peer_w_q'] = nrm((DEPTH, D_MODEL, PEER_HEADS * PEER_QDIM), D_MODEL ** -0.5)
    d['peer_keys1'] = nrm((DEPTH, PEER_NKEYS, PEER_HALF), PEER_HALF ** -0.5)
    d['peer_keys2'] = nrm((DEPTH, PEER_NKEYS, PEER_HALF), PEER_HALF ** -0.5)
    d['peer_u'] = nrm((DEPTH, PEER_EXPERTS, D_MODEL), D_MODEL ** -0.5)
    d['peer_v'] = nrm((DEPTH, PEER_EXPERTS, D_MODEL), 0.25)
    d['ple_w_proj'] = nrm((DEPTH, PLE_DIM, D_MODEL), PLE_DIM ** -0.5)
    d['ple_w_gate'] = nrm((DEPTH, D_MODEL, D_MODEL), D_MODEL ** -0.5)
    d['ple_b_gate'] = nrm((DEPTH, D_MODEL), 0.02)
    return d


def reference(x_prompt, x_sample, p_prompt, p_sample, state_gdn_recurrent, state_gdn_conv, state_conf_conv,
              norm_mix, norm_ffn, norm_ple, final_norm,
              gdn_w_in, gdn_conv_w, gdn_a_log, gdn_dt_bias, gdn_o_norm, gdn_w_out,
              conf_w_in, conf_b_in, conf_dw_w, conf_dw_b, conf_ln_g, conf_ln_b, conf_w_out, conf_b_out,
              peer_w_q, peer_keys1, peer_keys2, peer_u, peer_v,
              ple_w_proj, ple_w_gate, ple_b_gate):
    weights = (norm_mix, norm_ffn, norm_ple, final_norm,
               gdn_w_in, gdn_conv_w, gdn_a_log, gdn_dt_bias, gdn_o_norm, gdn_w_out,
               conf_w_in, conf_b_in, conf_dw_w, conf_dw_b, conf_ln_g, conf_ln_b, conf_w_out, conf_b_out,
               peer_w_q, peer_keys1, peer_keys2, peer_u, peer_v,
               ple_w_proj, ple_w_gate, ple_b_gate)
    bp = x_prompt.shape[0]
    dt = x_prompt.dtype
    zero_S = jnp.zeros((N_A_LAYERS, bp, GDN_V_HEADS, GDN_HEAD_K, GDN_HEAD_V), dt)
    zero_gbuf = jnp.zeros((N_A_LAYERS, bp, GDN_CONV - 1, GDN_CONV_DIM), dt)
    zero_cbuf = jnp.zeros((N_B_LAYERS, bp, CONF_K - 1, CONF_CH), dt)
    y_prompt, gS_p, gbuf_p, cbuf_p = decoder_trunk(x_prompt, p_prompt, zero_S, zero_gbuf, zero_cbuf, *weights)
    y_sample, gS_s, gbuf_s, cbuf_s = decoder_trunk(x_sample, p_sample, state_gdn_recurrent, state_gdn_conv,
                                                   state_conf_conv, *weights)
    return (y_prompt, y_sample, gS_p, gS_s, gbuf_p, gbuf_s, cbuf_p, cbuf_s)
```

```python
import functools
import math

import jax
import jax.numpy as jnp
from jax import lax
from jax.experimental import pallas as pl
from jax.experimental.pallas import tpu as pltpu

F32 = jnp.float32
BF16 = jnp.bfloat16
EPS = 1e-6

GDN_HEAD = 128
GDN_CHUNK = 64
PEER_HEADS = 8
PEER_NKEYS = 128
PEER_HALF = 128
PEER_TOPK = 16

LANES = 128
SUBLANES = 8
VMEM_LIMIT = 56 * 1024 * 1024

_CANDS = [(a, b) for a in range(PEER_TOPK) for b in range(PEER_TOPK) if (a + 1) * (b + 1) <= PEER_TOPK]
_NCAND_PAD = -(-len(_CANDS) // SUBLANES) * SUBLANES


def _cparams(*sem):
    return pltpu.CompilerParams(dimension_semantics=sem, vmem_limit_bytes=VMEM_LIMIT)


def _tile(n, target):
    t = min(n, target)
    assert n % t == 0, (n, t)
    return t


def _dot(a, b):
    return jnp.dot(a.astype(BF16), b.astype(BF16), preferred_element_type=F32)


def _dot_nt(a, b):
    return lax.dot_general(a.astype(BF16), b.astype(BF16), (((1,), (1,)), ((), ())),
                           preferred_element_type=F32)


def _rms(x, g):
    return x * lax.rsqrt(jnp.mean(x * x, axis=-1, keepdims=True) + EPS) * g


def _sigmoid(x):
    return 1.0 / (1.0 + jnp.exp(-x))


def _silu(x):
    return x * _sigmoid(x)


def _rms_mm_kernel(x_ref, g_ref, w_ref, o_ref, xn_ref):
    @pl.when(pl.program_id(1) == 0)
    def _():
        xn_ref[...] = _rms(x_ref[...], g_ref[...]).astype(BF16)

    o_ref[...] = jnp.dot(xn_ref[...], w_ref[...], preferred_element_type=F32)


def rms_matmul(x, g, w, n_cols, tn=512, tm=1024):
    n, d = x.shape
    tm = _tile(n, tm)
    return pl.pallas_call(
        _rms_mm_kernel,
        grid=(n // tm, n_cols // tn),
        in_specs=[pl.BlockSpec((tm, d), lambda i, j: (i, 0)),
                  pl.BlockSpec((1, d), lambda i, j: (0, 0)),
                  pl.BlockSpec((d, tn), lambda i, j: (0, j))],
        out_specs=pl.BlockSpec((tm, tn), lambda i, j: (i, j)),
        out_shape=jax.ShapeDtypeStruct((n, n_cols), F32),
        scratch_shapes=[pltpu.VMEM((tm, d), BF16)],
        compiler_params=_cparams("parallel", "arbitrary"),
        name="rms_matmul",
    )(x, g.reshape(1, d), w)


def _rms_mm_xn_kernel(x_ref, g_ref, w_ref, o_ref, xn_ref):
    @pl.when(pl.program_id(1) == 0)
    def _():
        xn_ref[...] = _rms(x_ref[...], g_ref[...]).astype(BF16)

    o_ref[...] = jnp.dot(xn_ref[...], w_ref[...], preferred_element_type=F32)


def rms_matmul_xn(x, g, w, tn=512, tm=1024):
    n, d = x.shape
    n_cols = w.shape[1]
    tm = _tile(n, tm)
    return pl.pallas_call(
        _rms_mm_xn_kernel,
        grid=(n // tm, n_cols // tn),
        in_specs=[pl.BlockSpec((tm, d), lambda i, j: (i, 0)),
                  pl.BlockSpec((1, d), lambda i, j: (0, 0)),
                  pl.BlockSpec((d, tn), lambda i, j: (0, j))],
        out_specs=[pl.BlockSpec((tm, tn), lambda i, j: (i, j)),
                   pl.BlockSpec((tm, d), lambda i, j: (i, 0))],
        out_shape=[jax.ShapeDtypeStruct((n, n_cols), F32),
                   jax.ShapeDtypeStruct((n, d), BF16)],
        compiler_params=_cparams("parallel", "arbitrary"),
        name="rms_matmul_xn",
    )(x, g.reshape(1, d), w)


def _mm_res_kernel(x_ref, w_ref, b_ref, r_ref, o_ref):
    o_ref[...] = (jnp.dot(x_ref[...], w_ref[...], preferred_element_type=F32)
                  + b_ref[...] + r_ref[...])


def matmul_bias_res(x, w, b, res, tn=512, tm=1024):
    n, k = x.shape
    n_cols = w.shape[1]
    tm = _tile(n, tm)
    return pl.pallas_call(
        _mm_res_kernel,
        grid=(n // tm, n_cols // tn),
        in_specs=[pl.BlockSpec((tm, k), lambda i, j: (i, 0)),
                  pl.BlockSpec((k, tn), lambda i, j: (0, j)),
                  pl.BlockSpec((1, tn), lambda i, j: (0, j)),
                  pl.BlockSpec((tm, tn), lambda i, j: (i, j))],
        out_specs=pl.BlockSpec((tm, tn), lambda i, j: (i, j)),
        out_shape=jax.ShapeDtypeStruct((n, n_cols), F32),
        compiler_params=_cparams("parallel", "arbitrary"),
        name="matmul_bias_res",
    )(x, w, b.reshape(1, n_cols), res)


def _glu_kernel(x_ref, g_ref, wa_ref, wb_ref, ba_ref, bb_ref, o_ref, xn_ref):
    @pl.when(pl.program_id(1) == 0)
    def _():
        xn_ref[...] = _rms(x_ref[...], g_ref[...]).astype(BF16)

    a = jnp.dot(xn_ref[...], wa_ref[...], preferred_element_type=F32) + ba_ref[...]
    b = jnp.dot(xn_ref[...], wb_ref[...], preferred_element_type=F32) + bb_ref[...]
    o_ref[...] = a * _sigmoid(b)


def rms_matmul_glu(x, g, w, b, tn=512, tm=1024):
    n, d = x.shape
    c = w.shape[1] // 2
    tm = _tile(n, tm)
    nj = c // tn
    b2 = b.reshape(1, 2 * c)
    return pl.pallas_call(
        _glu_kernel,
        grid=(n // tm, nj),
        in_specs=[pl.BlockSpec((tm, d), lambda i, j: (i, 0)),
                  pl.BlockSpec((1, d), lambda i, j: (0, 0)),
                  pl.BlockSpec((d, tn), lambda i, j: (0, j)),
                  pl.BlockSpec((d, tn), lambda i, j: (0, j + nj)),
                  pl.BlockSpec((1, tn), lambda i, j: (0, j)),
                  pl.BlockSpec((1, tn), lambda i, j: (0, j + nj))],
        out_specs=pl.BlockSpec((tm, tn), lambda i, j: (i, j)),
        out_shape=jax.ShapeDtypeStruct((n, c), F32),
        scratch_shapes=[pltpu.VMEM((tm, d), BF16)],
        compiler_params=_cparams("parallel", "arbitrary"),
        name="rms_matmul_glu",
    )(x, g.reshape(1, d), w, w, b2, b2)


def _ple_kernel(h_ref, y_ref, g_ref, wg_ref, bg_ref, p_ref, wp_ref, hj_ref, yj_ref, o_ref, xn_ref):
    @pl.when(pl.program_id(1) == 0)
    def _():
        xn_ref[...] = _rms(h_ref[...] + y_ref[...], g_ref[...]).astype(BF16)

    gate = _sigmoid(jnp.dot(xn_ref[...], wg_ref[...], preferred_element_type=F32) + bg_ref[...])
    proj = jnp.dot(p_ref[...].astype(BF16), wp_ref[...], preferred_element_type=F32)
    o_ref[...] = hj_ref[...] + yj_ref[...] + proj * gate


def ple_mix(h, y, g, w_gate, b_gate, p, w_proj, tn=512, tm=512):
    n, d = h.shape
    pd = p.shape[1]
    tm = _tile(n, tm)
    return pl.pallas_call(
        _ple_kernel,
        grid=(n // tm, d // tn),
        in_specs=[pl.BlockSpec((tm, d), lambda i, j: (i, 0)),
                  pl.BlockSpec((tm, d), lambda i, j: (i, 0)),
                  pl.BlockSpec((1, d), lambda i, j: (0, 0)),
                  pl.BlockSpec((d, tn), lambda i, j: (0, j)),
                  pl.BlockSpec((1, tn), lambda i, j: (0, j)),
                  pl.BlockSpec((tm, pd), lambda i, j: (i, 0)),
                  pl.BlockSpec((pd, tn), lambda i, j: (0, j)),
                  pl.BlockSpec((tm, tn), lambda i, j: (i, j)),
                  pl.BlockSpec((tm, tn), lambda i, j: (i, j))],
        out_specs=pl.BlockSpec((tm, tn), lambda i, j: (i, j)),
        out_shape=jax.ShapeDtypeStruct((n, d), F32),
        scratch_shapes=[pltpu.VMEM((tm, d), BF16)],
        compiler_params=_cparams("parallel", "arbitrary"),
        name="ple_mix",
    )(h, y, g.reshape(1, d), w_gate, b_gate.reshape(1, d), p, w_proj, h, y)


def _rmsnorm_kernel(x_ref, g_ref, o_ref):
    o_ref[...] = _rms(x_ref[...], g_ref[...])


def rmsnorm(x, g, tm=512):
    n, d = x.shape
    tm = _tile(n, tm)
    return pl.pallas_call(
        _rmsnorm_kernel,
        grid=(n // tm,),
        in_specs=[pl.BlockSpec((tm, d), lambda i: (i, 0)),
                  pl.BlockSpec((1, d), lambda i: (0, 0))],
        out_specs=pl.BlockSpec((tm, d), lambda i: (i, 0)),
        out_shape=jax.ShapeDtypeStruct((n, d), F32),
        compiler_params=_cparams("parallel"),
        name="final_rmsnorm",
    )(x, g.reshape(1, d))


def _gdn_gates_kernel(x_ref, g_ref, wb_ref, wa_ref, alog_ref, dtb_ref, beta_ref, gc_ref, *, chunk):
    xn = _rms(x_ref[...], g_ref[...]).astype(BF16)
    b_raw = jnp.dot(xn, wb_ref[...], preferred_element_type=F32)
    a_raw = jnp.dot(xn, wa_ref[...], preferred_element_type=F32)
    beta_ref[...] = _sigmoid(b_raw)
    z = a_raw + dtb_ref[...]
    softplus = jnp.maximum(z, 0.0) + jnp.log(1.0 + jnp.exp(-jnp.abs(z)))
    gc = -jnp.exp(alog_ref[...]) * softplus
    pos = lax.broadcasted_iota(jnp.int32, gc.shape, 0) % chunk
    s = 1
    while s < chunk:
        gc = gc + jnp.where(pos >= s, pltpu.roll(gc, s, 0), 0.0)
        s *= 2
    gc_ref[...] = gc


def gdn_gates(x, g, w_b, w_a, a_log, dt_bias, chunk, tm=512):
    n, d = x.shape
    hv = w_b.shape[1]
    tm = _tile(n, tm)
    assert tm % chunk == 0
    return pl.pallas_call(
        functools.partial(_gdn_gates_kernel, chunk=chunk),
        grid=(n // tm,),
        in_specs=[pl.BlockSpec((tm, d), lambda i: (i, 0)),
                  pl.BlockSpec((1, d), lambda i: (0, 0)),
                  pl.BlockSpec((d, hv), lambda i: (0, 0)),
                  pl.BlockSpec((d, hv), lambda i: (0, 0)),
                  pl.BlockSpec((1, hv), lambda i: (0, 0)),
                  pl.BlockSpec((1, hv), lambda i: (0, 0))],
        out_specs=[pl.BlockSpec((tm, hv), lambda i: (i, 0)),
                   pl.BlockSpec((tm, hv), lambda i: (i, 0))],
        out_shape=[jax.ShapeDtypeStruct((n, hv), F32),
                   jax.ShapeDtypeStruct((n, hv), F32)],
        compiler_params=_cparams("parallel"),
        name="gdn_gates",
    )(x, g.reshape(1, d), w_b, w_a, a_log.reshape(1, hv), dt_bias.reshape(1, hv))


def _gdn_chunk(q, k, v, bet, gcb, s_prev, c):
    row = lax.broadcasted_iota(jnp.int32, (c, c), 0)
    col = lax.broadcasted_iota(jnp.int32, (c, c), 1)
    gi = gcb[:, :c]
    gj = gcb.T[:c, :]
    decay = jnp.exp(jnp.where(row >= col, gi - gj, -jnp.inf))
    egc = jnp.exp(gcb)
    kb = k * bet
    vb = v * bet
    lmat = jnp.where(row > col, _dot_nt(kb, k) * decay, 0.0)
    p = -lmat
    x = jnp.where(row == col, 1.0, 0.0) + p
    for _ in range(int(math.log2(c)) - 1):
        p = _dot(p, p)
        x = x + _dot(x, p)
    u = _dot(x, vb)
    w = _dot(x, kb * egc)
    v_new = u - _dot(w, s_prev)
    attn = _dot_nt(q, k) * decay
    o = _dot(q * egc, s_prev) + _dot(attn, v_new)
    glast = gcb[c - 1:c, :]
    kdec = k * jnp.exp(glast - gcb)
    s_new = s_prev * jnp.exp(glast) + _dot(kdec.T, v_new)
    return o, s_new


def _gdn_delta_kernel(*refs, t, c, bb, has_state):
    if has_state:
        (q_ref, k_ref, v_ref, z_ref, beta_ref, gc_ref, cw_q_ref, cw_k_ref, cw_v_ref, onorm_ref,
         pq_ref, pk_ref, pv_ref, s0_ref, o_ref, s_ref,
         qs_ref, ks_ref, vs_ref, bs_ref, gs_ref, st_ref) = refs
    else:
        (q_ref, k_ref, v_ref, z_ref, beta_ref, gc_ref, cw_q_ref, cw_k_ref, cw_v_ref, onorm_ref,
         o_ref, s_ref,
         qs_ref, ks_ref, vs_ref, bs_ref, gs_ref, st_ref) = refs
    hk = pl.program_id(1)
    n_chunks = t // c
    kconv = cw_q_ref.shape[0]

    def conv_silu(x, w_ref, prev):
        tpos = lax.broadcasted_iota(jnp.int32, x.shape, 0)
        y = x * w_ref[kconv - 1:kconv, :]
        for s in range(1, kconv):
            if prev is None:
                tail = 0.0
            else:
                tail = pltpu.roll(prev, s, 0)
            shifted = jnp.where(tpos >= s, pltpu.roll(x, s, 0), tail)
            y = y + shifted * w_ref[kconv - 1 - s:kconv - s, :]
        return _silu(y)

    def l2n(x):
        return x * lax.rsqrt(jnp.sum(x * x, axis=-1, keepdims=True) + EPS)

    def per_batch(bi, carry):
        pq = pk = pv = None
        if has_state:
            pq, pk, pv = pq_ref[bi], pk_ref[bi], pv_ref[bi]
        qs_ref[...] = l2n(conv_silu(q_ref[bi], cw_q_ref, pq)) * (GDN_HEAD ** -0.5)
        ks_ref[...] = l2n(conv_silu(k_ref[bi], cw_k_ref, pk))
        vs_ref[...] = conv_silu(v_ref[bi], cw_v_ref, pv)
        lane = lax.broadcasted_iota(jnp.int32, (t, beta_ref.shape[2]), 1)
        for i in range(2):
            sel = lane == (2 * hk + i)
            bcol = jnp.sum(jnp.where(sel, beta_ref[bi], 0.0), axis=1, keepdims=True)
            gcol = jnp.sum(jnp.where(sel, gc_ref[bi], 0.0), axis=1, keepdims=True)
            bs_ref[i] = jnp.broadcast_to(bcol, (t, GDN_HEAD))
            gs_ref[i] = jnp.broadcast_to(gcol, (t, GDN_HEAD))
            if has_state:
                st_ref[i] = s0_ref[bi, i]
            else:
                st_ref[i] = jnp.zeros((GDN_HEAD, GDN_HEAD), F32)

        def per_chunk(n, carry2):
            r0 = pl.multiple_of(n * c, c)
            q = qs_ref[pl.ds(r0, c), :]
            k = ks_ref[pl.ds(r0, c), :]
            for i in range(2):
                v = vs_ref[pl.ds(r0, c), i * GDN_HEAD:(i + 1) * GDN_HEAD]
                o, s_new = _gdn_chunk(q, k, v, bs_ref[i, pl.ds(r0, c), :], gs_ref[i, pl.ds(r0, c), :],
                                      st_ref[i], c)
                st_ref[i] = s_new
                z = z_ref[bi, pl.ds(r0, c), i * GDN_HEAD:(i + 1) * GDN_HEAD]
                gated = _rms(o, onorm_ref[...]) * _silu(z)
                o_ref[bi, pl.ds(r0, c), i * GDN_HEAD:(i + 1) * GDN_HEAD] = gated.astype(BF16)
            return carry2

        lax.fori_loop(0, n_chunks, per_chunk, 0)
        for i in range(2):
            s_ref[bi, i] = st_ref[i]
        return carry

    lax.fori_loop(0, bb, per_batch, 0)


def gdn_delta(proj, beta, gc, conv_w, o_norm, conv_state, s0, bb):
    b, t, _ = proj.shape
    hv = beta.shape[2]
    hq = hv // 2
    c = min(GDN_CHUNK, t)
    assert t % c == 0 and b % bb == 0
    has_state = s0 is not None
    kc = conv_w.shape[0]
    hd = GDN_HEAD
    q0, k0, v0, z0 = 0, hq, hq, 2 * hq

    in_specs = [pl.BlockSpec((bb, t, hd), lambda i, h: (i, 0, q0 + h)),
                pl.BlockSpec((bb, t, hd), lambda i, h: (i, 0, k0 + h)),
                pl.BlockSpec((bb, t, 2 * hd), lambda i, h: (i, 0, v0 + h)),
                pl.BlockSpec((bb, t, 2 * hd), lambda i, h: (i, 0, z0 + h)),
                pl.BlockSpec((bb, t, hv), lambda i, h: (i, 0, 0)),
                pl.BlockSpec((bb, t, hv), lambda i, h: (i, 0, 0)),
                pl.BlockSpec((kc, hd), lambda i, h: (0, q0 + h)),
                pl.BlockSpec((kc, hd), lambda i, h: (0, k0 + h)),
                pl.BlockSpec((kc, 2 * hd), lambda i, h: (0, v0 + h)),
                pl.BlockSpec((1, hd), lambda i, h: (0, 0))]
    args = [proj, proj, proj, proj, beta, gc, conv_w, conv_w, conv_w, o_norm.reshape(1, hd)]
    if has_state:
        in_specs += [pl.BlockSpec((bb, SUBLANES, hd), lambda i, h: (i, 0, q0 + h)),
                     pl.BlockSpec((bb, SUBLANES, hd), lambda i, h: (i, 0, k0 + h)),
                     pl.BlockSpec((bb, SUBLANES, 2 * hd), lambda i, h: (i, 0, v0 + h)),
                     pl.BlockSpec((bb, 2, hd, hd), lambda i, h: (i, h, 0, 0))]
        args += [conv_state, conv_state, conv_state, s0]
    return pl.pallas_call(
        functools.partial(_gdn_delta_kernel, t=t, c=c, bb=bb, has_state=has_state),
        grid=(b // bb, hq),
        in_specs=in_specs,
        out_specs=[pl.BlockSpec((bb, t, 2 * hd), lambda i, h: (i, 0, h)),
                   pl.BlockSpec((bb, 2, hd, hd), lambda i, h: (i, h, 0, 0))],
        out_shape=[jax.ShapeDtypeStruct((b, t, hv * hd), BF16),
                   jax.ShapeDtypeStruct((b, hv, hd, hd), F32)],
        scratch_shapes=[pltpu.VMEM((t, hd), F32), pltpu.VMEM((t, hd), F32),
                        pltpu.VMEM((t, 2 * hd), F32),
                        pltpu.VMEM((2, t, hd), F32), pltpu.VMEM((2, t, hd), F32),
                        pltpu.VMEM((2, hd, hd), F32)],
        compiler_params=_cparams("parallel", "parallel"),
        name="gdn_delta",
    )(*args)


def _conf_conv_kernel(x_ref, halo_ref, w_ref, b_ref, lg_ref, lb_ref, o_ref, xx_ref, *, bb, tm, kw, hpad):
    base = hpad - (kw - 1)

    def per_batch(bi, carry):
        xx_ref[0:hpad, :] = halo_ref[bi]
        xx_ref[hpad:hpad + tm, :] = x_ref[bi]
        acc = xx_ref[base:base + tm, :] * w_ref[0:1, :]
        for j in range(1, kw):
            acc = acc + xx_ref[base + j:base + j + tm, :] * w_ref[j:j + 1, :]
        acc = acc + b_ref[...]
        mu = jnp.mean(acc, axis=-1, keepdims=True)
        xc = acc - mu
        y = xc * lax.rsqrt(jnp.mean(xc * xc, axis=-1, keepdims=True) + EPS)
        y = y * lg_ref[...] + lb_ref[...]
        o_ref[bi] = _silu(y).astype(BF16)
        return carry

    lax.fori_loop(0, bb, per_batch, 0)


def conf_conv(x, halo, w, b, ln_g, ln_b, bb):
    nb, tm, ch = x.shape
    hpad = halo.shape[1]
    kw = w.shape[0]
    assert nb % bb == 0 and hpad >= kw - 1
    return pl.pallas_call(
        functools.partial(_conf_conv_kernel, bb=bb, tm=tm, kw=kw, hpad=hpad),
        grid=(nb // bb,),
        in_specs=[pl.BlockSpec((bb, tm, ch), lambda i: (i, 0, 0)),
                  pl.BlockSpec((bb, hpad, ch), lambda i: (i, 0, 0)),
                  pl.BlockSpec((kw, ch), lambda i: (0, 0)),
                  pl.BlockSpec((1, ch), lambda i: (0, 0)),
                  pl.BlockSpec((1, ch), lambda i: (0, 0)),
                  pl.BlockSpec((1, ch), lambda i: (0, 0))],
        out_specs=pl.BlockSpec((bb, tm, ch), lambda i: (i, 0, 0)),
        out_shape=jax.ShapeDtypeStruct((nb, tm, ch), BF16),
        scratch_shapes=[pltpu.VMEM((hpad + tm, ch), F32)],
        compiler_params=_cparams("parallel"),
        name="conf_conv",
    )(x, halo, w, b.reshape(1, ch), ln_g.reshape(1, ch), ln_b.reshape(1, ch))


def _extract_top(s, rounds):
    rows = s.shape[0]
    iota = lax.broadcasted_iota(jnp.int32, s.shape, 0)
    rank = jnp.full(s.shape, float(rounds), F32)
    vals = []
    for r in range(rounds):
        m = jnp.max(s, axis=0, keepdims=True)
        idx = jnp.min(jnp.where(s == m, iota, rows), axis=0, keepdims=True)
        sel = iota == idx
        rank = jnp.where(sel, float(r), rank)
        s = jnp.where(sel, -jnp.inf, s)
        vals.append(m)
    return rank, vals


def _peer_route_kernel(q_ref, k1_ref, k2_ref, r2_ref, e2_ref, lc_ref, co_ref, cand_ref):
    s1 = _dot_nt(k1_ref[...], q_ref[:, :PEER_HALF])
    s2 = _dot_nt(k2_ref[...], q_ref[:, PEER_HALF:])
    rank1, v1 = _extract_top(s1, PEER_TOPK)
    rank2, v2 = _extract_top(s2, PEER_TOPK)
    cand_ref[...] = jnp.full(cand_ref.shape, -jnp.inf, F32)
    for r, (a, b) in enumerate(_CANDS):
        cand_ref[r:r + 1, :] = v1[a] + v2[b]
    cand = cand_ref[...]
    crank, cvals = _extract_top(cand, PEER_TOPK)
    sel = jnp.where(crank < float(PEER_TOPK), 1.0, 0.0)
    zsum = jnp.sum(sel * jnp.exp(cand - cvals[0]), axis=0, keepdims=True)
    crow = lax.broadcasted_iota(jnp.int32, cand.shape, 0)
    lc = jnp.zeros_like(s1)
    pos = 0
    for a in range(PEER_TOPK):
        cnt = sum(1 for (aa, _) in _CANDS if aa == a)
        in_a = (crow >= pos) & (crow < pos + cnt)
        la = jnp.sum(jnp.where(in_a, sel, 0.0), axis=0, keepdims=True)
        lc = lc + jnp.where(rank1 == float(a), la, 0.0)
        pos += cnt
    r2_ref[0] = rank2
    e2_ref[0] = jnp.exp(s2 - v2[0])
    lc_ref[0] = lc
    co_ref[0] = jnp.exp(s1 - v1[0]) / zsum


def peer_route(q, keys1, keys2, tm=512):
    n, qd = q.shape
    tm = _tile(n, tm)
    nk = keys1.shape[0]
    assert qd == PEER_HEADS * 2 * PEER_HALF
    tab = jax.ShapeDtypeStruct((PEER_HEADS, nk, n), F32)
    tspec = pl.BlockSpec((1, nk, tm), lambda i, h: (h, 0, i))
    return pl.pallas_call(
        _peer_route_kernel,
        grid=(n // tm, PEER_HEADS),
        in_specs=[pl.BlockSpec((tm, 2 * PEER_HALF), lambda i, h: (i, h)),
                  pl.BlockSpec(keys1.shape, lambda i, h: (0, 0)),
                  pl.BlockSpec(keys2.shape, lambda i, h: (0, 0))],
        out_specs=[tspec, tspec, tspec, tspec],
        out_shape=[tab, tab, tab, tab],
        scratch_shapes=[pltpu.VMEM((_NCAND_PAD, tm), F32)],
        compiler_params=_cparams("parallel", "parallel"),
        name="peer_route",
    )(q, keys1, keys2)


def _peer_mix_kernel(xn_ref, u_ref, vt_ref, r2_ref, e2_ref, lc_ref, co_ref, o_ref, acc_ref, *, eb):
    j = pl.program_id(1)

    @pl.when(j == 0)
    def _():
        acc_ref[...] = jnp.zeros_like(acc_ref)

    ht = _dot_nt(u_ref[...], xn_ref[...])
    act = 0.5 * ht * (1.0 + lax.erf(ht * (2.0 ** -0.5)))
    parts = []
    nk = PEER_NKEYS
    for ii in range(eb // nk):
        i1 = j * (eb // nk) + ii
        gate = jnp.zeros((nk, ht.shape[1]), F32)
        for h in range(PEER_HEADS):
            lc = lc_ref[h, pl.ds(i1, 1), :]
            co = co_ref[h, pl.ds(i1, 1), :]
            gate = gate + jnp.where(r2_ref[h] < lc, e2_ref[h] * co, 0.0)
        parts.append((gate * act[ii * nk:(ii + 1) * nk, :]).astype(BF16))
    a = parts[0] if len(parts) == 1 else jnp.concatenate(parts, axis=0)
    acc_ref[...] += jnp.dot(vt_ref[...], a, preferred_element_type=F32)

    @pl.when(j == pl.num_programs(1) - 1)
    def _():
        o_ref[...] = acc_ref[...].T


def peer_mix(xn, u, vt, r2, e2, lc, co, tm=512, eb=512):
    n, d = xn.shape
    ne = u.shape[0]
    tm = _tile(n, tm)
    tspec = pl.BlockSpec((PEER_HEADS, PEER_NKEYS, tm), lambda i, j: (0, 0, i))
    return pl.pallas_call(
        functools.partial(_peer_mix_kernel, eb=eb),
        grid=(n // tm, ne // eb),
        in_specs=[pl.BlockSpec((tm, d), lambda i, j: (i, 0)),
                  pl.BlockSpec((eb, d), lambda i, j: (j, 0)),
                  pl.BlockSpec((d, eb), lambda i, j: (0, j)),
                  tspec, tspec, tspec, tspec],
        out_specs=pl.BlockSpec((tm, d), lambda i, j: (i, 0)),
        out_shape=jax.ShapeDtypeStruct((n, d), F32),
        scratch_shapes=[pltpu.VMEM((d, tm), F32)],
        compiler_params=_cparams("parallel", "arbitrary"),
        name="peer_mix",
    )(xn, u, vt, r2, e2, lc, co)


def _prep_weights(norm_mix, norm_ffn, norm_ple, final_norm,
                  gdn_w_in, gdn_conv_w, gdn_a_log, gdn_dt_bias, gdn_o_norm, gdn_w_out,
                  conf_w_in, conf_b_in, conf_dw_w, conf_dw_b, conf_ln_g, conf_ln_b, conf_w_out, conf_b_out,
                  peer_w_q, peer_keys1, peer_keys2, peer_u, peer_v,
                  ple_w_proj, ple_w_gate, ple_b_gate):
    hv = gdn_a_log.shape[1]
    main = gdn_w_in.shape[2] - 2 * hv
    return dict(
        gdn_w_main=gdn_w_in[:, :, :main].astype(BF16),
        gdn_w_b=gdn_w_in[:, :, main:main + hv].astype(BF16),
        gdn_w_a=gdn_w_in[:, :, main + hv:].astype(BF16),
        gdn_w_out=gdn_w_out.astype(BF16),
        conf_w_in=conf_w_in.astype(BF16),
        conf_w_out=conf_w_out.astype(BF16),
        peer_w_q=peer_w_q.astype(BF16),
        peer_keys1=peer_keys1.astype(BF16),
        peer_keys2=peer_keys2.astype(BF16),
        peer_u=peer_u.astype(BF16),
        peer_vt=jnp.swapaxes(peer_v.astype(BF16), 1, 2),
        ple_w_proj=ple_w_proj.astype(BF16),
        ple_w_gate=ple_w_gate.astype(BF16),
    )


def _trunk(x, p, gdn_s, gdn_buf, conf_buf, wts, pw):
    (norm_mix, norm_ffn, norm_ple, final_norm,
     gdn_w_in, gdn_conv_w, gdn_a_log, gdn_dt_bias, gdn_o_norm, gdn_w_out,
     conf_w_in, conf_b_in, conf_dw_w, conf_dw_b, conf_ln_g, conf_ln_b, conf_w_out, conf_b_out,
     peer_w_q, peer_keys1, peer_keys2, peer_u, peer_v,
     ple_w_proj, ple_w_gate, ple_b_gate) = wts
    b, t, d = x.shape
    n = b * t
    depth = norm_mix.shape[0]
    has_state = gdn_s is not None
    h = x.reshape(n, d)
    new_s, new_gbuf, new_cbuf = [], [], []
    for i in range(depth):
        j = i // 2
        if i % 2 == 0:
            hv = gdn_a_log.shape[1]
            conv_dim = gdn_conv_w.shape[2]
            kc = gdn_conv_w.shape[1]
            main = pw["gdn_w_main"].shape[2]
            proj = rms_matmul(h, norm_mix[i], pw["gdn_w_main"][j], main)
            chunk = min(GDN_CHUNK, t)
            beta, gc = gdn_gates(h, norm_mix[i], pw["gdn_w_b"][j], pw["gdn_w_a"][j],
                                 gdn_a_log[j], gdn_dt_bias[j], chunk)
            proj3 = proj.reshape(b, t, main)
            if has_state:
                cstate = jnp.pad(gdn_buf[j], ((0, 0), (SUBLANES - (kc - 1), 0), (0, 0)))
                o, s_fin = gdn_delta(proj3, beta.reshape(b, t, hv), gc.reshape(b, t, hv), gdn_conv_w[j],
                                     gdn_o_norm[j], cstate, gdn_s[j], bb=16)
            else:
                o, s_fin = gdn_delta(proj3, beta.reshape(b, t, hv), gc.reshape(b, t, hv), gdn_conv_w[j],
                                     gdn_o_norm[j], None, None, bb=1)
            new_s.append(s_fin)
            new_gbuf.append(proj3[:, t - (kc - 1):, :conv_dim])
            h = matmul_bias_res(o.reshape(n, hv * GDN_HEAD), pw["gdn_w_out"][j],
                                jnp.zeros((d,), F32), h)
        else:
            kw = conf_dw_w.shape[1]
            ch = conf_dw_w.shape[2]
            glu = rms_matmul_glu(h, norm_mix[i], pw["conf_w_in"][j], conf_b_in[j])
            glu3 = glu.reshape(b, t, ch)
            hpad = 32
            if has_state:
                prev = conf_buf[j]
                halo = jnp.pad(prev, ((0, 0), (hpad - (kw - 1), 0), (0, 0)))
                c = conf_conv(glu3, halo, conf_dw_w[j], conf_dw_b[j], conf_ln_g[j], conf_ln_b[j], bb=8)
                new_cbuf.append(jnp.concatenate([prev, glu3], axis=1)[:, t:])
            else:
                tmc = _tile(t, 256)
                nt = t // tmc
                tiles = glu3.reshape(b, nt, tmc, ch)
                halo = jnp.concatenate([jnp.zeros((b, 1, hpad, ch), F32), tiles[:, :-1, tmc - hpad:]], axis=1)
                c = conf_conv(tiles.reshape(b * nt, tmc, ch), halo.reshape(b * nt, hpad, ch),
                              conf_dw_w[j], conf_dw_b[j], conf_ln_g[j], conf_ln_b[j], bb=1)
                new_cbuf.append(glu3[:, t - (kw - 1):])
            h = matmul_bias_res(c.reshape(n, ch), pw["conf_w_out"][j], conf_b_out[j], h)
        q, xn = rms_matmul_xn(h, norm_ffn[i], pw["peer_w_q"][i])
        r2, e2, lc, co = peer_route(q, pw["peer_keys1"][i], pw["peer_keys2"][i])
        y = peer_mix(xn, pw["peer_u"][i], pw["peer_vt"][i], r2, e2, lc, co)
        h = ple_mix(h, y, norm_ple[i], pw["ple_w_gate"][i], ple_b_gate[i],
                    p[i].reshape(n, -1), pw["ple_w_proj"][i])
    out = rmsnorm(h, final_norm).reshape(b, t, d)
    return out, jnp.stack(new_s), jnp.stack(new_gbuf), jnp.stack(new_cbuf)


def kernel(x_prompt, x_sample, p_prompt, p_sample, state_gdn_recurrent, state_gdn_conv, state_conf_conv, norm_mix, norm_ffn, norm_ple, final_norm, gdn_w_in, gdn_conv_w, gdn_a_log, gdn_dt_bias, gdn_o_norm, gdn_w_out, conf_w_in, conf_b_in, conf_dw_w, conf_dw_b, conf_ln_g, conf_ln_b, conf_w_out, conf_b_out, peer_w_q, peer_keys1, peer_keys2, peer_u, peer_v, ple_w_proj, ple_w_gate, ple_b_gate):
    wts = (norm_mix, norm_ffn, norm_ple, final_norm,
           gdn_w_in, gdn_conv_w, gdn_a_log, gdn_dt_bias, gdn_o_norm, gdn_w_out,
           conf_w_in, conf_b_in, conf_dw_w, conf_dw_b, conf_ln_g, conf_ln_b, conf_w_out, conf_b_out,
           peer_w_q, peer_keys1, peer_keys2, peer_u, peer_v,
           ple_w_proj, ple_w_gate, ple_b_gate)
    pw = _prep_weights(*wts)
    y_p, s_p, gb_p, cb_p = _trunk(x_prompt, p_prompt, None, None, None, wts, pw)
    y_s, s_s, gb_s, cb_s = _trunk(x_sample, p_sample, state_gdn_recurrent, state_gdn_conv,
                                  state_conf_conv, wts, pw)
    return (y_p, y_s, s_p, s_s, gb_p, gb_s, cb_p, cb_s)
```

```python
import functools
import math

import jax
import jax.numpy as jnp
from jax import lax
from jax.experimental import pallas as pl
from jax.experimental.pallas import tpu as pltpu

F32 = jnp.float32
BF16 = jnp.bfloat16
EPS = 1e-6

GDN_HEAD = 128
GDN_CHUNK = 64
PEER_HEADS = 8
PEER_NKEYS = 128
PEER_HALF = 128
PEER_TOPK = 16

LANES = 128
SUBLANES = 8
VMEM_LIMIT = 56 * 1024 * 1024

_CANDS = [(a, b) for a in range(PEER_TOPK) for b in range(PEER_TOPK) if (a + 1) * (b + 1) <= PEER_TOPK]
_NCAND_PAD = -(-len(_CANDS) // SUBLANES) * SUBLANES


def _cparams(*sem):
    return pltpu.CompilerParams(dimension_semantics=sem, vmem_limit_bytes=VMEM_LIMIT)


def _tile(n, target):
    t = min(n, target)
    assert n % t == 0, (n, t)
    return t


def _dot(a, b):
    return jnp.dot(a.astype(BF16), b.astype(BF16), preferred_element_type=F32)


def _dot_nt(a, b):
    return lax.dot_general(a.astype(BF16), b.astype(BF16), (((1,), (1,)), ((), ())),
                           preferred_element_type=F32)


def _rms(x, g):
    return x * lax.rsqrt(jnp.mean(x * x, axis=-1, keepdims=True) + EPS) * g


def _sigmoid(x):
    return 1.0 / (1.0 + jnp.exp(-x))


def _silu(x):
    return x * _sigmoid(x)


def _rms_mm_kernel(x_ref, g_ref, w_ref, o_ref, xn_ref):
    @pl.when(pl.program_id(1) == 0)
    def _():
        xn_ref[...] = _rms(x_ref[...], g_ref[...]).astype(BF16)

    o_ref[...] = jnp.dot(xn_ref[...], w_ref[...], preferred_element_type=F32)


def rms_matmul(x, g, w, n_cols, tn=512, tm=1024):
    n, d = x.shape
    tm = _tile(n, tm)
    return pl.pallas_call(
        _rms_mm_kernel,
        grid=(n // tm, n_cols // tn),
        in_specs=[pl.BlockSpec((tm, d), lambda i, j: (i, 0)),
                  pl.BlockSpec((1, d), lambda i, j: (0, 0)),
                  pl.BlockSpec((d, tn), lambda i, j: (0, j))],
        out_specs=pl.BlockSpec((tm, tn), lambda i, j: (i, j)),
        out_shape=jax.ShapeDtypeStruct((n, n_cols), F32),
        scratch_shapes=[pltpu.VMEM((tm, d), BF16)],
        compiler_params=_cparams("parallel", "arbitrary"),
        name="rms_matmul",
    )(x, g.reshape(1, d), w)


def rms_matmul_xn(x, g, w, tn=512, tm=1024):
    n, d = x.shape
    n_cols = w.shape[1]
    tm = _tile(n, tm)
    return pl.pallas_call(
        _rms_mm_kernel,
        grid=(n // tm, n_cols // tn),
        in_specs=[pl.BlockSpec((tm, d), lambda i, j: (i, 0)),
                  pl.BlockSpec((1, d), lambda i, j: (0, 0)),
                  pl.BlockSpec((d, tn), lambda i, j: (0, j))],
        out_specs=[pl.BlockSpec((tm, tn), lambda i, j: (i, j)),
                   pl.BlockSpec((tm, d), lambda i, j: (i, 0))],
        out_shape=[jax.ShapeDtypeStruct((n, n_cols), F32),
                   jax.ShapeDtypeStruct((n, d), BF16)],
        compiler_params=_cparams("parallel", "arbitrary"),
        name="rms_matmul_xn",
    )(x, g.reshape(1, d), w)


def _mm_res_kernel(x_ref, w_ref, b_ref, r_ref, o_ref):
    o_ref[...] = (jnp.dot(x_ref[...], w_ref[...], preferred_element_type=F32)
                  + b_ref[...] + r_ref[...])


def matmul_bias_res(x, w, b, res, tn=512, tm=1024):
    n, k = x.shape
    n_cols = w.shape[1]
    tm = _tile(n, tm)
    return pl.pallas_call(
        _mm_res_kernel,
        grid=(n // tm, n_cols // tn),
        in_specs=[pl.BlockSpec((tm, k), lambda i, j: (i, 0)),
                  pl.BlockSpec((k, tn), lambda i, j: (0, j)),
                  pl.BlockSpec((1, tn), lambda i, j: (0, j)),
                  pl.BlockSpec((tm, tn), lambda i, j: (i, j))],
        out_specs=pl.BlockSpec((tm, tn), lambda i, j: (i, j)),
        out_shape=jax.ShapeDtypeStruct((n, n_cols), F32),
        compiler_params=_cparams("parallel", "arbitrary"),
        name="matmul_bias_res",
    )(x, w, b.reshape(1, n_cols), res)


def _glu_kernel(x_ref, g_ref, wa_ref, wb_ref, ba_ref, bb_ref, o_ref, xn_ref):
    @pl.when(pl.program_id(1) == 0)
    def _():
        xn_ref[...] = _rms(x_ref[...], g_ref[...]).astype(BF16)

    a = jnp.dot(xn_ref[...], wa_ref[...], preferred_element_type=F32) + ba_ref[...]
    b = jnp.dot(xn_ref[...], wb_ref[...], preferred_element_type=F32) + bb_ref[...]
    o_ref[...] = a * _sigmoid(b)


def rms_matmul_glu(x, g, w, b, tn=512, tm=1024):
    n, d = x.shape
    c = w.shape[1] // 2
    tm = _tile(n, tm)
    nj = c // tn
    b2 = b.reshape(1, 2 * c)
    return pl.pallas_call(
        _glu_kernel,
        grid=(n // tm, nj),
        in_specs=[pl.BlockSpec((tm, d), lambda i, j: (i, 0)),
                  pl.BlockSpec((1, d), lambda i, j: (0, 0)),
                  pl.BlockSpec((d, tn), lambda i, j: (0, j)),
                  pl.BlockSpec((d, tn), lambda i, j: (0, j + nj)),
                  pl.BlockSpec((1, tn), lambda i, j: (0, j)),
                  pl.BlockSpec((1, tn), lambda i, j: (0, j + nj))],
        out_specs=pl.BlockSpec((tm, tn), lambda i, j: (i, j)),
        out_shape=jax.ShapeDtypeStruct((n, c), F32),
        scratch_shapes=[pltpu.VMEM((tm, d), BF16)],
        compiler_params=_cparams("parallel", "arbitrary"),
        name="rms_matmul_glu",
    )(x, g.reshape(1, d), w, w, b2, b2)


def _ple_kernel(h_ref, y_ref, g_ref, wg_ref, bg_ref, p_ref, wp_ref, hj_ref, yj_ref, o_ref, xn_ref):
    @pl.when(pl.program_id(1) == 0)
    def _():
        xn_ref[...] = _rms(h_ref[...] + y_ref[...], g_ref[...]).astype(BF16)

    gate = _sigmoid(jnp.dot(xn_ref[...], wg_ref[...], preferred_element_type=F32) + bg_ref[...])
    proj = jnp.dot(p_ref[...].astype(BF16), wp_ref[...], preferred_element_type=F32)
    o_ref[...] = hj_ref[...] + yj_ref[...] + proj * gate


def ple_mix(h, y, g, w_gate, b_gate, p, w_proj, tn=512, tm=512):
    n, d = h.shape
    pd = p.shape[1]
    tm = _tile(n, tm)
    return pl.pallas_call(
        _ple_kernel,
        grid=(n // tm, d // tn),
        in_specs=[pl.BlockSpec((tm, d), lambda i, j: (i, 0)),
                  pl.BlockSpec((tm, d), lambda i, j: (i, 0)),
                  pl.BlockSpec((1, d), lambda i, j: (0, 0)),
                  pl.BlockSpec((d, tn), lambda i, j: (0, j)),
                  pl.BlockSpec((1, tn), lambda i, j: (0, j)),
                  pl.BlockSpec((tm, pd), lambda i, j: (i, 0)),
                  pl.BlockSpec((pd, tn), lambda i, j: (0, j)),
                  pl.BlockSpec((tm, tn), lambda i, j: (i, j)),
                  pl.BlockSpec((tm, tn), lambda i, j: (i, j))],
        out_specs=pl.BlockSpec((tm, tn), lambda i, j: (i, j)),
        out_shape=jax.ShapeDtypeStruct((n, d), F32),
        scratch_shapes=[pltpu.VMEM((tm, d), BF16)],
        compiler_params=_cparams("parallel", "arbitrary"),
        name="ple_mix",
    )(h, y, g.reshape(1, d), w_gate, b_gate.reshape(1, d), p, w_proj, h, y)


def _rmsnorm_kernel(x_ref, g_ref, o_ref):
    o_ref[...] = _rms(x_ref[...], g_ref[...])


def rmsnorm(x, g, tm=512):
    n, d = x.shape
    tm = _tile(n, tm)
    return pl.pallas_call(
        _rmsnorm_kernel,
        grid=(n // tm,),
        in_specs=[pl.BlockSpec((tm, d), lambda i: (i, 0)),
                  pl.BlockSpec((1, d), lambda i: (0, 0))],
        out_specs=pl.BlockSpec((tm, d), lambda i: (i, 0)),
        out_shape=jax.ShapeDtypeStruct((n, d), F32),
        compiler_params=_cparams("parallel"),
        name="final_rmsnorm",
    )(x, g.reshape(1, d))


def _gdn_gates_kernel(x_ref, g_ref, wb_ref, wa_ref, alog_ref, dtb_ref, beta_ref, gc_ref, *, chunk):
    xn = _rms(x_ref[...], g_ref[...]).astype(BF16)
    b_raw = jnp.dot(xn, wb_ref[...], preferred_element_type=F32)
    a_raw = jnp.dot(xn, wa_ref[...], preferred_element_type=F32)
    beta_ref[...] = _sigmoid(b_raw)
    z = a_raw + dtb_ref[...]
    softplus = jnp.maximum(z, 0.0) + jnp.log(1.0 + jnp.exp(-jnp.abs(z)))
    gc = -jnp.exp(alog_ref[...]) * softplus
    pos = lax.broadcasted_iota(jnp.int32, gc.shape, 0) % chunk
    s = 1
    while s < chunk:
        gc = gc + jnp.where(pos >= s, pltpu.roll(gc, s, 0), 0.0)
        s *= 2
    gc_ref[...] = gc


def gdn_gates(x, g, w_b, w_a, a_log, dt_bias, chunk, tm=512):
    n, d = x.shape
    hv = w_b.shape[1]
    tm = _tile(n, tm)
    assert tm % chunk == 0
    return pl.pallas_call(
        functools.partial(_gdn_gates_kernel, chunk=chunk),
        grid=(n // tm,),
        in_specs=[pl.BlockSpec((tm, d), lambda i: (i, 0)),
                  pl.BlockSpec((1, d), lambda i: (0, 0)),
                  pl.BlockSpec((d, hv), lambda i: (0, 0)),
                  pl.BlockSpec((d, hv), lambda i: (0, 0)),
                  pl.BlockSpec((1, hv), lambda i: (0, 0)),
                  pl.BlockSpec((1, hv), lambda i: (0, 0))],
        out_specs=[pl.BlockSpec((tm, hv), lambda i: (i, 0)),
                   pl.BlockSpec((tm, hv), lambda i: (i, 0))],
        out_shape=[jax.ShapeDtypeStruct((n, hv), F32),
                   jax.ShapeDtypeStruct((n, hv), F32)],
        compiler_params=_cparams("parallel"),
        name="gdn_gates",
    )(x, g.reshape(1, d), w_b, w_a, a_log.reshape(1, hv), dt_bias.reshape(1, hv))


def _conv_silu(x, w_ref, prev):
    kconv = w_ref.shape[0]
    tpos = lax.broadcasted_iota(jnp.int32, x.shape, 0)
    y = x * w_ref[kconv - 1:kconv, :]
    for s in range(1, kconv):
        tail = 0.0 if prev is None else pltpu.roll(prev, s, 0)
        shifted = jnp.where(tpos >= s, pltpu.roll(x, s, 0), tail)
        y = y + shifted * w_ref[kconv - 1 - s:kconv - s, :]
    return _silu(y)


def _l2n(x):
    return x * lax.rsqrt(jnp.sum(x * x, axis=-1, keepdims=True) + EPS)


def _head_column(x, head):
    lane = lax.broadcasted_iota(jnp.int32, x.shape, 1)
    col = jnp.sum(jnp.where(lane == head, x, 0.0), axis=1, keepdims=True)
    return jnp.broadcast_to(col, (x.shape[0], GDN_HEAD))


def _gdn_chunk_pre(chains, c):
    row = lax.broadcasted_iota(jnp.int32, (c, c), 0)
    col = lax.broadcasted_iota(jnp.int32, (c, c), 1)
    eye = jnp.where(row == col, 1.0, 0.0)
    st = []
    for (q, k, v, bet, gcb) in chains:
        gi = gcb[:, :c]
        gj = gcb.T[:c, :]
        decay = jnp.exp(jnp.where(row >= col, gi - gj, -jnp.inf))
        egc = jnp.exp(gcb)
        kb = k * bet
        st.append(dict(q=q, k=k, gcb=gcb, decay=decay, egc=egc, kb=kb, vb=v * bet))
    for s in st:
        s["kq"] = _dot_nt(jnp.concatenate([s["kb"], s["q"]], axis=0), s["k"])
    for s in st:
        s["attn"] = s["kq"][c:] * s["decay"]
        s["p"] = -jnp.where(row > col, s["kq"][:c] * s["decay"], 0.0)
        s["x"] = eye + s["p"]
    for s in st:
        s["p"] = _dot(s["p"], s["p"])
    for _ in range(int(math.log2(c)) - 2):
        for s in st:
            s["m"] = _dot(jnp.concatenate([s["p"], s["x"]], axis=0), s["p"])
        for s in st:
            s["p"] = s["m"][:c]
            s["x"] = s["x"] + s["m"][c:]
    for s in st:
        s["m"] = _dot(s["x"], s["p"])
    for s in st:
        s["x"] = s["x"] + s["m"]
        s["rhs"] = jnp.concatenate([s["vb"], s["kb"] * s["egc"]], axis=1)
    for s in st:
        s["uw"] = _dot(s["x"], s["rhs"])
    out = []
    for s in st:
        glast = s["gcb"][c - 1:c, :]
        kdect = (s["k"] * jnp.exp(glast - s["gcb"])).T
        wq = jnp.concatenate([s["uw"][:, GDN_HEAD:], s["q"] * s["egc"]], axis=0)
        ak = jnp.concatenate([s["attn"], kdect], axis=0)
        out.append((s["uw"][:, :GDN_HEAD], wq, ak, jnp.exp(glast)))
    return out


def _gdn_chunk_step(chains, c):
    rs = [_dot(wq, s_prev) for (u, wq, ak, eg, s_prev) in chains]
    v_news = [u - r[:c] for (u, wq, ak, eg, s_prev), r in zip(chains, rs)]
    ms = [_dot(ak, v_new) for (u, wq, ak, eg, s_prev), v_new in zip(chains, v_news)]
    return [(r[c:] + m[:c], s_prev * eg + m[c:])
            for (u, wq, ak, eg, s_prev), r, m in zip(chains, rs, ms)]


def _gdn_seq_kernel(q_ref, k_ref, v_ref, z_ref, beta_ref, gc_ref, cw_q_ref, cw_k_ref, cw_v_ref,
                    onorm_ref, o_ref, s_ref,
                    qs_ref, ks_ref, vs_ref, bs_ref, gs_ref, u_ref, wq_ref, ak_ref, eg_ref, st_ref,
                    *, t, c, group):
    hk = pl.program_id(1)
    n_chunks = t // c
    hd = GDN_HEAD
    qs_ref[...] = _l2n(_conv_silu(q_ref[0], cw_q_ref, None)) * (hd ** -0.5)
    ks_ref[...] = _l2n(_conv_silu(k_ref[0], cw_k_ref, None))
    vs_ref[...] = _conv_silu(v_ref[0], cw_v_ref, None)
    for i in range(2):
        bs_ref[i] = _head_column(beta_ref[0], 2 * hk + i)
        gs_ref[i] = _head_column(gc_ref[0], 2 * hk + i)
        st_ref[i] = jnp.zeros((hd, hd), F32)

    def pre(it, carry):
        loaded = []
        for g in range(group):
            n = it * group + g
            r0 = pl.multiple_of(n * c, c)
            q = qs_ref[pl.ds(r0, c), :]
            k = ks_ref[pl.ds(r0, c), :]
            for i in range(2):
                loaded.append((n, i, q, k, vs_ref[pl.ds(r0, c), i * hd:(i + 1) * hd],
                               bs_ref[i, pl.ds(r0, c), :], gs_ref[i, pl.ds(r0, c), :]))
        pre_out = _gdn_chunk_pre([x[2:] for x in loaded], c)
        for (n, i, *_), (u, wq, ak, eg) in zip(loaded, pre_out):
            u_ref[i, pl.ds(pl.multiple_of(n * c, c), c), :] = u
            wq_ref[i, pl.ds(pl.multiple_of(n * 2 * c, 2 * c), 2 * c), :] = wq.astype(BF16)
            ak_ref[i, pl.ds(pl.multiple_of(n * (c + hd), c + hd), c + hd), :] = ak.astype(BF16)
            eg_ref[i, pl.ds(pl.multiple_of(n * SUBLANES, SUBLANES), SUBLANES), :] = (
                jnp.broadcast_to(eg, (SUBLANES, hd)))
        return carry

    lax.fori_loop(0, n_chunks // group, pre, 0)

    def step(n, carry):
        r0 = pl.multiple_of(n * c, c)
        chains = []
        for i in range(2):
            chains.append((u_ref[i, pl.ds(r0, c), :],
                           wq_ref[i, pl.ds(pl.multiple_of(n * 2 * c, 2 * c), 2 * c), :],
                           ak_ref[i, pl.ds(pl.multiple_of(n * (c + hd), c + hd), c + hd), :],
                           eg_ref[i, pl.ds(pl.multiple_of(n * SUBLANES, SUBLANES), 1), :],
                           st_ref[i]))
        for i, (o, s_new) in enumerate(_gdn_chunk_step(chains, c)):
            st_ref[i] = s_new
            z = z_ref[0, pl.ds(r0, c), i * hd:(i + 1) * hd]
            gated = _rms(o, onorm_ref[...]) * _silu(z)
            o_ref[0, pl.ds(r0, c), i * hd:(i + 1) * hd] = gated.astype(BF16)
        return carry

    lax.fori_loop(0, n_chunks, step, 0)
    for i in range(2):
        s_ref[0, i] = st_ref[i]


def _gdn_step_kernel(q_ref, k_ref, v_ref, z_ref, beta_ref, gc_ref, cw_q_ref, cw_k_ref, cw_v_ref,
                     onorm_ref, pq_ref, pk_ref, pv_ref, s0_ref, o_ref, s_ref, *, c, bb, group):
    hk = pl.program_id(1)
    hd = GDN_HEAD

    def per_group(it, carry):
        loaded = []
        for g in range(group):
            bi = it * group + g
            loaded.append((bi, q_ref[bi], k_ref[bi], v_ref[bi], pq_ref[bi], pk_ref[bi], pv_ref[bi],
                           beta_ref[bi], gc_ref[bi], z_ref[bi], s0_ref[bi, 0], s0_ref[bi, 1]))
        chains, meta = [], []
        for (bi, xq, xk, xv, pq, pk, pv, beta, gc, z, s0a, s0b) in loaded:
            q = _l2n(_conv_silu(xq, cw_q_ref, pq)) * (hd ** -0.5)
            k = _l2n(_conv_silu(xk, cw_k_ref, pk))
            vv = _conv_silu(xv, cw_v_ref, pv)
            for i, s0 in enumerate((s0a, s0b)):
                chains.append((q, k, vv[:, i * hd:(i + 1) * hd],
                               _head_column(beta, 2 * hk + i), _head_column(gc, 2 * hk + i)))
                meta.append((bi, i, z[:, i * hd:(i + 1) * hd], s0))
        pre_out = _gdn_chunk_pre(chains, c)
        step_out = _gdn_chunk_step([p + (m[3],) for p, m in zip(pre_out, meta)], c)
        for (bi, i, z, _), (o, s_new) in zip(meta, step_out):
            s_ref[bi, i] = s_new
            gated = _rms(o, onorm_ref[...]) * _silu(z)
            o_ref[bi, :, i * hd:(i + 1) * hd] = gated.astype(BF16)
        return carry

    lax.fori_loop(0, bb // group, per_group, 0)


def gdn_delta(proj, beta, gc, conv_w, o_norm, conv_state, s0):
    b, t, _ = proj.shape
    hv = beta.shape[2]
    hq = hv // 2
    c = min(GDN_CHUNK, t)
    assert t % c == 0
    n_chunks = t // c
    kc = conv_w.shape[0]
    hd = GDN_HEAD
    q0, k0, v0, z0 = 0, hq, hq, 2 * hq
    bb = 1 if s0 is None else _tile(b, 16)

    in_specs = [pl.BlockSpec((bb, t, hd), lambda i, h: (i, 0, q0 + h)),
                pl.BlockSpec((bb, t, hd), lambda i, h: (i, 0, k0 + h)),
                pl.BlockSpec((bb, t, 2 * hd), lambda i, h: (i, 0, v0 + h)),
                pl.BlockSpec((bb, t, 2 * hd), lambda i, h: (i, 0, z0 + h)),
                pl.BlockSpec((bb, t, hv), lambda i, h: (i, 0, 0)),
                pl.BlockSpec((bb, t, hv), lambda i, h: (i, 0, 0)),
                pl.BlockSpec((kc, hd), lambda i, h: (0, q0 + h)),
                pl.BlockSpec((kc, hd), lambda i, h: (0, k0 + h)),
                pl.BlockSpec((kc, 2 * hd), lambda i, h: (0, v0 + h)),
                pl.BlockSpec((1, hd), lambda i, h: (0, 0))]
    args = [proj, proj, proj, proj, beta, gc, conv_w, conv_w, conv_w, o_norm.reshape(1, hd)]
    out_specs = [pl.BlockSpec((bb, t, 2 * hd), lambda i, h: (i, 0, h)),
                 pl.BlockSpec((bb, 2, hd, hd), lambda i, h: (i, h, 0, 0))]
    out_shape = [jax.ShapeDtypeStruct((b, t, hv * hd), BF16),
                 jax.ShapeDtypeStruct((b, hv, hd, hd), F32)]
    if s0 is None:
        group = 4 if n_chunks % 4 == 0 else 1
        body = functools.partial(_gdn_seq_kernel, t=t, c=c, group=group)
        scratch = [pltpu.VMEM((t, hd), F32), pltpu.VMEM((t, hd), F32), pltpu.VMEM((t, 2 * hd), F32),
                   pltpu.VMEM((2, t, hd), F32), pltpu.VMEM((2, t, hd), F32),
                   pltpu.VMEM((2, t, hd), F32),
                   pltpu.VMEM((2, n_chunks * 2 * c, hd), BF16),
                   pltpu.VMEM((2, n_chunks * (c + hd), c), BF16),
                   pltpu.VMEM((2, n_chunks * SUBLANES, hd), F32),
                   pltpu.VMEM((2, hd, hd), F32)]
    else:
        assert n_chunks == 1
        group = 8 if bb % 8 == 0 else 1
        body = functools.partial(_gdn_step_kernel, c=c, bb=bb, group=group)
        scratch = []
        in_specs += [pl.BlockSpec((bb, SUBLANES, hd), lambda i, h: (i, 0, q0 + h)),
                     pl.BlockSpec((bb, SUBLANES, hd), lambda i, h: (i, 0, k0 + h)),
                     pl.BlockSpec((bb, SUBLANES, 2 * hd), lambda i, h: (i, 0, v0 + h)),
                     pl.BlockSpec((bb, 2, hd, hd), lambda i, h: (i, h, 0, 0))]
        args += [conv_state, conv_state, conv_state, s0]
    return pl.pallas_call(
        body,
        grid=(b // bb, hq),
        in_specs=in_specs,
        out_specs=out_specs,
        out_shape=out_shape,
        scratch_shapes=scratch,
        compiler_params=_cparams("parallel", "parallel"),
        name="gdn_delta",
    )(*args)


def _conf_conv_kernel(x_ref, halo_ref, w_ref, b_ref, lg_ref, lb_ref, o_ref, xx_ref, *, bb, tm, kw, hpad):
    base = hpad - (kw - 1)

    def per_batch(bi, carry):
        xx_ref[0:hpad, :] = halo_ref[bi]
        xx_ref[hpad:hpad + tm, :] = x_ref[bi]
        acc = xx_ref[base:base + tm, :] * w_ref[0:1, :]
        for j in range(1, kw):
            acc = acc + xx_ref[base + j:base + j + tm, :] * w_ref[j:j + 1, :]
        acc = acc + b_ref[...]
        mu = jnp.mean(acc, axis=-1, keepdims=True)
        xc = acc - mu
        y = xc * lax.rsqrt(jnp.mean(xc * xc, axis=-1, keepdims=True) + EPS)
        y = y * lg_ref[...] + lb_ref[...]
        o_ref[bi] = _silu(y).astype(BF16)
        return carry

    lax.fori_loop(0, bb, per_batch, 0)


def conf_conv(x, halo, w, b, ln_g, ln_b, bb):
    nb, tm, ch = x.shape
    hpad = halo.shape[1]
    kw = w.shape[0]
    assert nb % bb == 0 and hpad >= kw - 1
    return pl.pallas_call(
        functools.partial(_conf_conv_kernel, bb=bb, tm=tm, kw=kw, hpad=hpad),
        grid=(nb // bb,),
        in_specs=[pl.BlockSpec((bb, tm, ch), lambda i: (i, 0, 0)),
                  pl.BlockSpec((bb, hpad, ch), lambda i: (i, 0, 0)),
                  pl.BlockSpec((kw, ch), lambda i: (0, 0)),
                  pl.BlockSpec((1, ch), lambda i: (0, 0)),
                  pl.BlockSpec((1, ch), lambda i: (0, 0)),
                  pl.BlockSpec((1, ch), lambda i: (0, 0))],
        out_specs=pl.BlockSpec((bb, tm, ch), lambda i: (i, 0, 0)),
        out_shape=jax.ShapeDtypeStruct((nb, tm, ch), BF16),
        scratch_shapes=[pltpu.VMEM((hpad + tm, ch), F32)],
        compiler_params=_cparams("parallel"),
        name="conf_conv",
    )(x, halo, w, b.reshape(1, ch), ln_g.reshape(1, ch), ln_b.reshape(1, ch))


def _extract_top(s, rounds):
    rows = s.shape[0]
    iota = lax.broadcasted_iota(jnp.int32, s.shape, 0)
    rank = jnp.full(s.shape, float(rounds), F32)
    vals = []
    for r in range(rounds):
        m = jnp.max(s, axis=0, keepdims=True)
        idx = jnp.min(jnp.where(s == m, iota, rows), axis=0, keepdims=True)
        sel = iota == idx
        rank = jnp.where(sel, float(r), rank)
        s = jnp.where(sel, -jnp.inf, s)
        vals.append(m)
    return rank, vals


def _peer_route_kernel(q_ref, k1_ref, k2_ref, r2_ref, e2_ref, lc_ref, co_ref, cand_ref):
    s1 = _dot_nt(k1_ref[...], q_ref[:, :PEER_HALF])
    s2 = _dot_nt(k2_ref[...], q_ref[:, PEER_HALF:])
    rank1, v1 = _extract_top(s1, PEER_TOPK)
    rank2, v2 = _extract_top(s2, PEER_TOPK)
    cand_ref[...] = jnp.full(cand_ref.shape, -jnp.inf, F32)
    for r, (a, b) in enumerate(_CANDS):
        cand_ref[r:r + 1, :] = v1[a] + v2[b]
    cand = cand_ref[...]
    crank, cvals = _extract_top(cand, PEER_TOPK)
    sel = jnp.where(crank < float(PEER_TOPK), 1.0, 0.0)
    zsum = jnp.sum(sel * jnp.exp(cand - cvals[0]), axis=0, keepdims=True)
    crow = lax.broadcasted_iota(jnp.int32, cand.shape, 0)
    lc = jnp.zeros_like(s1)
    pos = 0
    for a in range(PEER_TOPK):
        cnt = sum(1 for (aa, _) in _CANDS if aa == a)
        in_a = (crow >= pos) & (crow < pos + cnt)
        la = jnp.sum(jnp.where(in_a, sel, 0.0), axis=0, keepdims=True)
        lc = lc + jnp.where(rank1 == float(a), la, 0.0)
        pos += cnt
    r2_ref[0] = rank2
    e2_ref[0] = jnp.exp(s2 - v2[0])
    lc_ref[0] = lc
    co_ref[0] = jnp.exp(s1 - v1[0]) / zsum


def peer_route(q, keys1, keys2, tm=512):
    n, qd = q.shape
    tm = _tile(n, tm)
    nk = keys1.shape[0]
    assert qd == PEER_HEADS * 2 * PEER_HALF
    tab = jax.ShapeDtypeStruct((PEER_HEADS, nk, n), F32)
    tspec = pl.BlockSpec((1, nk, tm), lambda i, h: (h, 0, i))
    return pl.pallas_call(
        _peer_route_kernel,
        grid=(n // tm, PEER_HEADS),
        in_specs=[pl.BlockSpec((tm, 2 * PEER_HALF), lambda i, h: (i, h)),
                  pl.BlockSpec(keys1.shape, lambda i, h: (0, 0)),
                  pl.BlockSpec(keys2.shape, lambda i, h: (0, 0))],
        out_specs=[tspec, tspec, tspec, tspec],
        out_shape=[tab, tab, tab, tab],
        scratch_shapes=[pltpu.VMEM((_NCAND_PAD, tm), F32)],
        compiler_params=_cparams("parallel", "parallel"),
        name="peer_route",
    )(q, keys1, keys2)


def _peer_mix_kernel(xn_ref, u_ref, vt_ref, r2_ref, e2_ref, lc_ref, co_ref, o_ref, acc_ref, a_ref, *, eb, sub):
    j = pl.program_id(1)

    @pl.when(j == 0)
    def _():
        acc_ref[...] = jnp.zeros_like(acc_ref)

    nk = PEER_NKEYS
    nsub = eb // sub

    def hidden(s):
        return _dot_nt(u_ref[s * sub:(s + 1) * sub, :], xn_ref[...])

    tm = xn_ref.shape[0]
    rt, ct = 64, LANES

    def gated(s, ht):
        for ii in range(sub // nk):
            i1 = j * (eb // nk) + s * (sub // nk) + ii
            lrows = [lc_ref[h, pl.ds(i1, 1), :] for h in range(PEER_HEADS)]
            crows = [co_ref[h, pl.ds(i1, 1), :] for h in range(PEER_HEADS)]
            for c0 in range(0, tm, ct):
                lcs = [x[:, c0:c0 + ct] for x in lrows]
                cos = [x[:, c0:c0 + ct] for x in crows]
                for r0 in range(0, nk, rt):
                    hv = ht[ii * nk + r0:ii * nk + r0 + rt, c0:c0 + ct]
                    act = 0.5 * hv * (1.0 + lax.erf(hv * (2.0 ** -0.5)))
                    gate = jnp.zeros((rt, ct), F32)
                    for h in range(PEER_HEADS):
                        gate = gate + jnp.where(r2_ref[h, r0:r0 + rt, c0:c0 + ct] < lcs[h],
                                                e2_ref[h, r0:r0 + rt, c0:c0 + ct] * cos[h], 0.0)
                    a_ref[s, ii * nk + r0:ii * nk + r0 + rt, c0:c0 + ct] = (gate * act).astype(BF16)

    hts = [hidden(s) for s in range(nsub)]
    for s in range(nsub):
        gated(s, hts[s])
        acc_ref[...] += jnp.dot(vt_ref[:, s * sub:(s + 1) * sub], a_ref[s], preferred_element_type=F32)

    @pl.when(j == pl.num_programs(1) - 1)
    def _():
        o_ref[...] = acc_ref[...].T


def peer_mix(xn, u, vt, r2, e2, lc, co, tm=512, eb=1024, sub=256):
    n, d = xn.shape
    ne = u.shape[0]
    tm = _tile(n, tm)
    tspec = pl.BlockSpec((PEER_HEADS, PEER_NKEYS, tm), lambda i, j: (0, 0, i))
    return pl.pallas_call(
        functools.partial(_peer_mix_kernel, eb=eb, sub=sub),
        grid=(n // tm, ne // eb),
        in_specs=[pl.BlockSpec((tm, d), lambda i, j: (i, 0)),
                  pl.BlockSpec((eb, d), lambda i, j: (j, 0)),
                  pl.BlockSpec((d, eb), lambda i, j: (0, j)),
                  tspec, tspec, tspec, tspec],
        out_specs=pl.BlockSpec((tm, d), lambda i, j: (i, 0)),
        out_shape=jax.ShapeDtypeStruct((n, d), F32),
        scratch_shapes=[pltpu.VMEM((d, tm), F32), pltpu.VMEM((eb // sub, sub, tm), BF16)],
        compiler_params=_cparams("parallel", "arbitrary"),
        name="peer_mix",
    )(xn, u, vt, r2, e2, lc, co)


def _prep_weights(norm_mix, norm_ffn, norm_ple, final_norm,
                  gdn_w_in, gdn_conv_w, gdn_a_log, gdn_dt_bias, gdn_o_norm, gdn_w_out,
                  conf_w_in, conf_b_in, conf_dw_w, conf_dw_b, conf_ln_g, conf_ln_b, conf_w_out, conf_b_out,
                  peer_w_q, peer_keys1, peer_keys2, peer_u, peer_v,
                  ple_w_proj, ple_w_gate, ple_b_gate):
    hv = gdn_a_log.shape[1]
    main = gdn_w_in.shape[2] - 2 * hv
    return dict(
        gdn_w_main=gdn_w_in[:, :, :main].astype(BF16),
        gdn_w_b=gdn_w_in[:, :, main:main + hv].astype(BF16),
        gdn_w_a=gdn_w_in[:, :, main + hv:].astype(BF16),
        gdn_w_out=gdn_w_out.astype(BF16),
        conf_w_in=conf_w_in.astype(BF16),
        conf_w_out=conf_w_out.astype(BF16),
        peer_w_q=peer_w_q.astype(BF16),
        peer_keys1=peer_keys1.astype(BF16),
        peer_keys2=peer_keys2.astype(BF16),
        peer_u=peer_u.astype(BF16),
        peer_vt=jnp.swapaxes(peer_v.astype(BF16), 1, 2),
        ple_w_proj=ple_w_proj.astype(BF16),
        ple_w_gate=ple_w_gate.astype(BF16),
    )


def _trunk(x, p, gdn_s, gdn_buf, conf_buf, wts, pw):
    (norm_mix, norm_ffn, norm_ple, final_norm,
     gdn_w_in, gdn_conv_w, gdn_a_log, gdn_dt_bias, gdn_o_norm, gdn_w_out,
     conf_w_in, conf_b_in, conf_dw_w, conf_dw_b, conf_ln_g, conf_ln_b, conf_w_out, conf_b_out,
     peer_w_q, peer_keys1, peer_keys2, peer_u, peer_v,
     ple_w_proj, ple_w_gate, ple_b_gate) = wts
    b, t, d = x.shape
    n = b * t
    depth = norm_mix.shape[0]
    has_state = gdn_s is not None
    h = x.reshape(n, d)
    new_s, new_gbuf, new_cbuf = [], [], []
    for i in range(depth):
        j = i // 2
        if i % 2 == 0:
            hv = gdn_a_log.shape[1]
            conv_dim = gdn_conv_w.shape[2]
            kc = gdn_conv_w.shape[1]
            main = pw["gdn_w_main"].shape[2]
            proj = rms_matmul(h, norm_mix[i], pw["gdn_w_main"][j], main)
            chunk = min(GDN_CHUNK, t)
            beta, gc = gdn_gates(h, norm_mix[i], pw["gdn_w_b"][j], pw["gdn_w_a"][j],
                                 gdn_a_log[j], gdn_dt_bias[j], chunk)
            proj3 = proj.reshape(b, t, main)
            if has_state:
                cstate = jnp.pad(gdn_buf[j], ((0, 0), (SUBLANES - (kc - 1), 0), (0, 0)))
                s0 = gdn_s[j]
            else:
                cstate, s0 = None, None
            o, s_fin = gdn_delta(proj3, beta.reshape(b, t, hv), gc.reshape(b, t, hv), gdn_conv_w[j],
                                 gdn_o_norm[j], cstate, s0)
            new_s.append(s_fin)
            new_gbuf.append(proj3[:, t - (kc - 1):, :conv_dim])
            h = matmul_bias_res(o.reshape(n, hv * GDN_HEAD), pw["gdn_w_out"][j],
                                jnp.zeros((d,), F32), h)
        else:
            kw = conf_dw_w.shape[1]
            ch = conf_dw_w.shape[2]
            glu = rms_matmul_glu(h, norm_mix[i], pw["conf_w_in"][j], conf_b_in[j])
            glu3 = glu.reshape(b, t, ch)
            hpad = 32
            if has_state:
                prev = conf_buf[j]
                halo = jnp.pad(prev, ((0, 0), (hpad - (kw - 1), 0), (0, 0)))
                c = conf_conv(glu3, halo, conf_dw_w[j], conf_dw_b[j], conf_ln_g[j], conf_ln_b[j], bb=8)
                new_cbuf.append(jnp.concatenate([prev, glu3], axis=1)[:, t:])
            else:
                tmc = _tile(t, 256)
                nt = t // tmc
                tiles = glu3.reshape(b, nt, tmc, ch)
                halo = jnp.concatenate([jnp.zeros((b, 1, hpad, ch), F32), tiles[:, :-1, tmc - hpad:]], axis=1)
                c = conf_conv(tiles.reshape(b * nt, tmc, ch), halo.reshape(b * nt, hpad, ch),
                              conf_dw_w[j], conf_dw_b[j], conf_ln_g[j], conf_ln_b[j], bb=1)
                new_cbuf.append(glu3[:, t - (kw - 1):])
            h = matmul_bias_res(c.reshape(n, ch), pw["conf_w_out"][j], conf_b_out[j], h)
        q, xn = rms_matmul_xn(h, norm_ffn[i], pw["peer_w_q"][i])
        r2, e2, lc, co = peer_route(q, pw["peer_keys1"][i], pw["peer_keys2"][i])
        y = peer_mix(xn, pw["peer_u"][i], pw["peer_vt"][i], r2, e2, lc, co)
        h = ple_mix(h, y, norm_ple[i], pw["ple_w_gate"][i], ple_b_gate[i],
                    p[i].reshape(n, -1), pw["ple_w_proj"][i])
    out = rmsnorm(h, final_norm).reshape(b, t, d)
    return out, jnp.stack(new_s), jnp.stack(new_gbuf), jnp.stack(new_cbuf)


def kernel(x_prompt, x_sample, p_prompt, p_sample, state_gdn_recurrent, state_gdn_conv, state_conf_conv, norm_mix, norm_ffn, norm_ple, final_norm, gdn_w_in, gdn_conv_w, gdn_a_log, gdn_dt_bias, gdn_o_norm, gdn_w_out, conf_w_in, conf_b_in, conf_dw_w, conf_dw_b, conf_ln_g, conf_ln_b, conf_w_out, conf_b_out, peer_w_q, peer_keys1, peer_keys2, peer_u, peer_v, ple_w_proj, ple_w_gate, ple_b_gate):
    wts = (norm_mix, norm_ffn, norm_ple, final_norm,
           gdn_w_in, gdn_conv_w, gdn_a_log, gdn_dt_bias, gdn_o_norm, gdn_w_out,
           conf_w_in, conf_b_in, conf_dw_w, conf_dw_b, conf_ln_g, conf_ln_b, conf_w_out, conf_b_out,
           peer_w_q, peer_keys1, peer_keys2, peer_u, peer_v,
           ple_w_proj, ple_w_gate, ple_b_gate)
    pw = _prep_weights(*wts)
    y_p, s_p, gb_p, cb_p = _trunk(x_prompt, p_prompt, None, None, None, wts, pw)
    y_s, s_s, gb_s, cb_s = _trunk(x_sample, p_sample, state_gdn_recurrent, state_gdn_conv,
                                  state_conf_conv, wts, pw)
    return (y_p, y_s, s_p, s_s, gb_p, gb_s, cb_p, cb_s)
```

```python
import functools
import math

import jax
import jax.numpy as jnp
from jax import lax
from jax.experimental import pallas as pl
from jax.experimental.pallas import tpu as pltpu

F32 = jnp.float32
BF16 = jnp.bfloat16
EPS = 1e-6

GDN_HEAD = 128
GDN_CHUNK = 64
PEER_HEADS = 8
PEER_NKEYS = 128
PEER_HALF = 128
PEER_TOPK = 16

LANES = 128
SUBLANES = 8
VMEM_LIMIT = 56 * 1024 * 1024

_CANDS = [(a, b) for a in range(PEER_TOPK) for b in range(PEER_TOPK) if (a + 1) * (b + 1) <= PEER_TOPK]
_NCAND_PAD = -(-len(_CANDS) // SUBLANES) * SUBLANES


def _cparams(*sem):
    return pltpu.CompilerParams(dimension_semantics=sem, vmem_limit_bytes=VMEM_LIMIT)


def _tile(n, target):
    t = min(n, target)
    assert n % t == 0, (n, t)
    return t


def _dot(a, b):
    return jnp.dot(a.astype(BF16), b.astype(BF16), preferred_element_type=F32)


def _dot_nt(a, b):
    return lax.dot_general(a.astype(BF16), b.astype(BF16), (((1,), (1,)), ((), ())),
                           preferred_element_type=F32)


def _rms(x, g):
    return x * lax.rsqrt(jnp.mean(x * x, axis=-1, keepdims=True) + EPS) * g


def _sigmoid(x):
    return 1.0 / (1.0 + jnp.exp(-x))


def _silu(x):
    return x * _sigmoid(x)


def _rms_mm_kernel(x_ref, g_ref, w_ref, o_ref, xn_ref):
    @pl.when(pl.program_id(1) == 0)
    def _():
        xn_ref[...] = _rms(x_ref[...], g_ref[...]).astype(BF16)

    o_ref[...] = jnp.dot(xn_ref[...], w_ref[...], preferred_element_type=F32)


def rms_matmul(x, g, w, n_cols, tn=512, tm=1024):
    n, d = x.shape
    tm = _tile(n, tm)
    return pl.pallas_call(
        _rms_mm_kernel,
        grid=(n // tm, n_cols // tn),
        in_specs=[pl.BlockSpec((tm, d), lambda i, j: (i, 0)),
                  pl.BlockSpec((1, d), lambda i, j: (0, 0)),
                  pl.BlockSpec((d, tn), lambda i, j: (0, j))],
        out_specs=pl.BlockSpec((tm, tn), lambda i, j: (i, j)),
        out_shape=jax.ShapeDtypeStruct((n, n_cols), F32),
        scratch_shapes=[pltpu.VMEM((tm, d), BF16)],
        compiler_params=_cparams("parallel", "arbitrary"),
        name="rms_matmul",
    )(x, g.reshape(1, d), w)


def rms_matmul_xn(x, g, w, tn=512, tm=1024):
    n, d = x.shape
    n_cols = w.shape[1]
    tm = _tile(n, tm)
    return pl.pallas_call(
        _rms_mm_kernel,
        grid=(n // tm, n_cols // tn),
        in_specs=[pl.BlockSpec((tm, d), lambda i, j: (i, 0)),
                  pl.BlockSpec((1, d), lambda i, j: (0, 0)),
                  pl.BlockSpec((d, tn), lambda i, j: (0, j))],
        out_specs=[pl.BlockSpec((tm, tn), lambda i, j: (i, j)),
                   pl.BlockSpec((tm, d), lambda i, j: (i, 0))],
        out_shape=[jax.ShapeDtypeStruct((n, n_cols), F32),
                   jax.ShapeDtypeStruct((n, d), BF16)],
        compiler_params=_cparams("parallel", "arbitrary"),
        name="rms_matmul_xn",
    )(x, g.reshape(1, d), w)


def _mm_res_kernel(x_ref, w_ref, b_ref, r_ref, o_ref):
    o_ref[...] = (jnp.dot(x_ref[...], w_ref[...], preferred_element_type=F32)
                  + b_ref[...] + r_ref[...])


def matmul_bias_res(x, w, b, res, tn=512, tm=1024):
    n, k = x.shape
    n_cols = w.shape[1]
    tm = _tile(n, tm)
    return pl.pallas_call(
        _mm_res_kernel,
        grid=(n // tm, n_cols // tn),
        in_specs=[pl.BlockSpec((tm, k), lambda i, j: (i, 0)),
                  pl.BlockSpec((k, tn), lambda i, j: (0, j)),
                  pl.BlockSpec((1, tn), lambda i, j: (0, j)),
                  pl.BlockSpec((tm, tn), lambda i, j: (i, j))],
        out_specs=pl.BlockSpec((tm, tn), lambda i, j: (i, j)),
        out_shape=jax.ShapeDtypeStruct((n, n_cols), F32),
        compiler_params=_cparams("parallel", "arbitrary"),
        name="matmul_bias_res",
    )(x, w, b.reshape(1, n_cols), res)


def _glu_kernel(x_ref, g_ref, wa_ref, wb_ref, ba_ref, bb_ref, o_ref, xn_ref):
    @pl.when(pl.program_id(1) == 0)
    def _():
        xn_ref[...] = _rms(x_ref[...], g_ref[...]).astype(BF16)

    a = jnp.dot(xn_ref[...], wa_ref[...], preferred_element_type=F32) + ba_ref[...]
    b = jnp.dot(xn_ref[...], wb_ref[...], preferred_element_type=F32) + bb_ref[...]
    o_ref[...] = a * _sigmoid(b)


def rms_matmul_glu(x, g, w, b, tn=512, tm=1024):
    n, d = x.shape
    c = w.shape[1] // 2
    tm = _tile(n, tm)
    nj = c // tn
    b2 = b.reshape(1, 2 * c)
    return pl.pallas_call(
        _glu_kernel,
        grid=(n // tm, nj),
        in_specs=[pl.BlockSpec((tm, d), lambda i, j: (i, 0)),
                  pl.BlockSpec((1, d), lambda i, j: (0, 0)),
                  pl.BlockSpec((d, tn), lambda i, j: (0, j)),
                  pl.BlockSpec((d, tn), lambda i, j: (0, j + nj)),
                  pl.BlockSpec((1, tn), lambda i, j: (0, j)),
                  pl.BlockSpec((1, tn), lambda i, j: (0, j + nj))],
        out_specs=pl.BlockSpec((tm, tn), lambda i, j: (i, j)),
        out_shape=jax.ShapeDtypeStruct((n, c), F32),
        scratch_shapes=[pltpu.VMEM((tm, d), BF16)],
        compiler_params=_cparams("parallel", "arbitrary"),
        name="rms_matmul_glu",
    )(x, g.reshape(1, d), w, w, b2, b2)


def _ple_kernel(h_ref, y_ref, g_ref, wg_ref, bg_ref, p_ref, wp_ref, hj_ref, yj_ref, o_ref, xn_ref):
    @pl.when(pl.program_id(1) == 0)
    def _():
        xn_ref[...] = _rms(h_ref[...] + y_ref[...], g_ref[...]).astype(BF16)

    gate = _sigmoid(jnp.dot(xn_ref[...], wg_ref[...], preferred_element_type=F32) + bg_ref[...])
    proj = jnp.dot(p_ref[...].astype(BF16), wp_ref[...], preferred_element_type=F32)
    o_ref[...] = hj_ref[...] + yj_ref[...] + proj * gate


def ple_mix(h, y, g, w_gate, b_gate, p, w_proj, tn=512, tm=512):
    n, d = h.shape
    pd = p.shape[1]
    tm = _tile(n, tm)
    return pl.pallas_call(
        _ple_kernel,
        grid=(n // tm, d // tn),
        in_specs=[pl.BlockSpec((tm, d), lambda i, j: (i, 0)),
                  pl.BlockSpec((tm, d), lambda i, j: (i, 0)),
                  pl.BlockSpec((1, d), lambda i, j: (0, 0)),
                  pl.BlockSpec((d, tn), lambda i, j: (0, j)),
                  pl.BlockSpec((1, tn), lambda i, j: (0, j)),
                  pl.BlockSpec((tm, pd), lambda i, j: (i, 0)),
                  pl.BlockSpec((pd, tn), lambda i, j: (0, j)),
                  pl.BlockSpec((tm, tn), lambda i, j: (i, j)),
                  pl.BlockSpec((tm, tn), lambda i, j: (i, j))],
        out_specs=pl.BlockSpec((tm, tn), lambda i, j: (i, j)),
        out_shape=jax.ShapeDtypeStruct((n, d), F32),
        scratch_shapes=[pltpu.VMEM((tm, d), BF16)],
        compiler_params=_cparams("parallel", "arbitrary"),
        name="ple_mix",
    )(h, y, g.reshape(1, d), w_gate, b_gate.reshape(1, d), p, w_proj, h, y)


def _rmsnorm_kernel(x_ref, g_ref, o_ref):
    o_ref[...] = _rms(x_ref[...], g_ref[...])


def rmsnorm(x, g, tm=512):
    n, d = x.shape
    tm = _tile(n, tm)
    return pl.pallas_call(
        _rmsnorm_kernel,
        grid=(n // tm,),
        in_specs=[pl.BlockSpec((tm, d), lambda i: (i, 0)),
                  pl.BlockSpec((1, d), lambda i: (0, 0))],
        out_specs=pl.BlockSpec((tm, d), lambda i: (i, 0)),
        out_shape=jax.ShapeDtypeStruct((n, d), F32),
        compiler_params=_cparams("parallel"),
        name="final_rmsnorm",
    )(x, g.reshape(1, d))


def _gdn_gates_kernel(x_ref, g_ref, wb_ref, wa_ref, alog_ref, dtb_ref, beta_ref, gc_ref, *, chunk):
    xn = _rms(x_ref[...], g_ref[...]).astype(BF16)
    b_raw = jnp.dot(xn, wb_ref[...], preferred_element_type=F32)
    a_raw = jnp.dot(xn, wa_ref[...], preferred_element_type=F32)
    beta_ref[...] = _sigmoid(b_raw)
    z = a_raw + dtb_ref[...]
    softplus = jnp.maximum(z, 0.0) + jnp.log(1.0 + jnp.exp(-jnp.abs(z)))
    gc = -jnp.exp(alog_ref[...]) * softplus
    pos = lax.broadcasted_iota(jnp.int32, gc.shape, 0) % chunk
    s = 1
    while s < chunk:
        gc = gc + jnp.where(pos >= s, pltpu.roll(gc, s, 0), 0.0)
        s *= 2
    gc_ref[...] = gc


def gdn_gates(x, g, w_b, w_a, a_log, dt_bias, chunk, tm=512):
    n, d = x.shape
    hv = w_b.shape[1]
    tm = _tile(n, tm)
    assert tm % chunk == 0
    return pl.pallas_call(
        functools.partial(_gdn_gates_kernel, chunk=chunk),
        grid=(n // tm,),
        in_specs=[pl.BlockSpec((tm, d), lambda i: (i, 0)),
                  pl.BlockSpec((1, d), lambda i: (0, 0)),
                  pl.BlockSpec((d, hv), lambda i: (0, 0)),
                  pl.BlockSpec((d, hv), lambda i: (0, 0)),
                  pl.BlockSpec((1, hv), lambda i: (0, 0)),
                  pl.BlockSpec((1, hv), lambda i: (0, 0))],
        out_specs=[pl.BlockSpec((tm, hv), lambda i: (i, 0)),
                   pl.BlockSpec((tm, hv), lambda i: (i, 0))],
        out_shape=[jax.ShapeDtypeStruct((n, hv), F32),
                   jax.ShapeDtypeStruct((n, hv), F32)],
        compiler_params=_cparams("parallel"),
        name="gdn_gates",
    )(x, g.reshape(1, d), w_b, w_a, a_log.reshape(1, hv), dt_bias.reshape(1, hv))


def _conv_silu(x, w_ref, prev):
    kconv = w_ref.shape[0]
    tpos = lax.broadcasted_iota(jnp.int32, x.shape, 0)
    y = x * w_ref[kconv - 1:kconv, :]
    for s in range(1, kconv):
        tail = 0.0 if prev is None else pltpu.roll(prev, s, 0)
        shifted = jnp.where(tpos >= s, pltpu.roll(x, s, 0), tail)
        y = y + shifted * w_ref[kconv - 1 - s:kconv - s, :]
    return _silu(y)


def _l2n(x):
    return x * lax.rsqrt(jnp.sum(x * x, axis=-1, keepdims=True) + EPS)


def _head_column(x, head):
    lane = lax.broadcasted_iota(jnp.int32, x.shape, 1)
    col = jnp.sum(jnp.where(lane == head, x, 0.0), axis=1, keepdims=True)
    return jnp.broadcast_to(col, (x.shape[0], GDN_HEAD))


def _gdn_chunk_pre(chains, c):
    row = lax.broadcasted_iota(jnp.int32, (c, c), 0)
    col = lax.broadcasted_iota(jnp.int32, (c, c), 1)
    eye = jnp.where(row == col, 1.0, 0.0)
    st = []
    for (q, k, v, bet, gcb) in chains:
        gi = gcb[:, :c]
        gj = gcb.T[:c, :]
        decay = jnp.exp(jnp.where(row >= col, gi - gj, -jnp.inf))
        egc = jnp.exp(gcb)
        kb = k * bet
        st.append(dict(q=q, k=k, gcb=gcb, decay=decay, egc=egc, kb=kb, vb=v * bet))
    for s in st:
        s["kq"] = _dot_nt(jnp.concatenate([s["kb"], s["q"]], axis=0), s["k"])
    for s in st:
        s["attn"] = s["kq"][c:] * s["decay"]
        s["p"] = -jnp.where(row > col, s["kq"][:c] * s["decay"], 0.0)
        s["x"] = eye + s["p"]
    for s in st:
        s["p"] = _dot(s["p"], s["p"])
    for _ in range(int(math.log2(c)) - 2):
        for s in st:
            s["m"] = _dot(jnp.concatenate([s["p"], s["x"]], axis=0), s["p"])
        for s in st:
            s["p"] = s["m"][:c]
            s["x"] = s["x"] + s["m"][c:]
    for s in st:
        s["m"] = _dot(s["x"], s["p"])
    for s in st:
        s["x"] = s["x"] + s["m"]
        s["rhs"] = jnp.concatenate([s["vb"], s["kb"] * s["egc"]], axis=1)
    for s in st:
        s["uw"] = _dot(s["x"], s["rhs"])
    for s in st:
        s["glast"] = s["gcb"][c - 1:c, :]
        kdect = (s["k"] * jnp.exp(s["glast"] - s["gcb"])).T
        s["t"] = _dot(jnp.concatenate([kdect, s["attn"]], axis=0), s["uw"])
    out = []
    hd = GDN_HEAD
    for s in st:
        t = s["t"]
        m = jnp.concatenate([-t[:hd, hd:], s["q"] * s["egc"] - t[hd:, hd:]], axis=0)
        out.append((m, t[:hd, :hd], t[hd:, :hd], jnp.exp(s["glast"])))
    return out


def _gdn_chunk_step(chains, c):
    rs = [_dot(m, s_prev) for (m, b, d, eg, s_prev) in chains]
    return [(r[GDN_HEAD:] + d, s_prev * eg + r[:GDN_HEAD] + b)
            for (m, b, d, eg, s_prev), r in zip(chains, rs)]


def _gdn_seq_kernel(q_ref, k_ref, v_ref, z_ref, beta_ref, gc_ref, cw_q_ref, cw_k_ref, cw_v_ref,
                    onorm_ref, o_ref, s_ref,
                    qs_ref, ks_ref, vs_ref, bs_ref, gs_ref, m_ref, b_ref, d_ref, eg_ref, st_ref,
                    *, t, c, group):
    hk = pl.program_id(1)
    n_chunks = t // c
    hd = GDN_HEAD
    qs_ref[...] = _l2n(_conv_silu(q_ref[0], cw_q_ref, None)) * (hd ** -0.5)
    ks_ref[...] = _l2n(_conv_silu(k_ref[0], cw_k_ref, None))
    vs_ref[...] = _conv_silu(v_ref[0], cw_v_ref, None)
    for i in range(2):
        bs_ref[i] = _head_column(beta_ref[0], 2 * hk + i)
        gs_ref[i] = _head_column(gc_ref[0], 2 * hk + i)
        st_ref[i] = jnp.zeros((hd, hd), F32)

    def pre(it, carry):
        loaded = []
        for g in range(group):
            n = it * group + g
            r0 = pl.multiple_of(n * c, c)
            q = qs_ref[pl.ds(r0, c), :]
            k = ks_ref[pl.ds(r0, c), :]
            for i in range(2):
                loaded.append((n, i, q, k, vs_ref[pl.ds(r0, c), i * hd:(i + 1) * hd],
                               bs_ref[i, pl.ds(r0, c), :], gs_ref[i, pl.ds(r0, c), :]))
        pre_out = _gdn_chunk_pre([x[2:] for x in loaded], c)
        for (n, i, *_), (m, b, d, eg) in zip(loaded, pre_out):
            m_ref[i, pl.ds(pl.multiple_of(n * (c + hd), c + hd), c + hd), :] = m.astype(BF16)
            b_ref[i, pl.ds(pl.multiple_of(n * hd, hd), hd), :] = b
            d_ref[i, pl.ds(pl.multiple_of(n * c, c), c), :] = d
            eg_ref[i, pl.ds(pl.multiple_of(n * SUBLANES, SUBLANES), SUBLANES), :] = (
                jnp.broadcast_to(eg, (SUBLANES, hd)))
        return carry

    lax.fori_loop(0, n_chunks // group, pre, 0)

    def step(n, carry):
        r0 = pl.multiple_of(n * c, c)
        chains = []
        for i in range(2):
            chains.append((m_ref[i, pl.ds(pl.multiple_of(n * (c + hd), c + hd), c + hd), :],
                           b_ref[i, pl.ds(pl.multiple_of(n * hd, hd), hd), :],
                           d_ref[i, pl.ds(r0, c), :],
                           eg_ref[i, pl.ds(pl.multiple_of(n * SUBLANES, SUBLANES), 1), :],
                           st_ref[i]))
        for i, (o, s_new) in enumerate(_gdn_chunk_step(chains, c)):
            st_ref[i] = s_new
            d_ref[i, pl.ds(r0, c), :] = o
        return carry

    lax.fori_loop(0, n_chunks, step, 0)

    for i in range(2):
        s_ref[0, i] = st_ref[i]
        gated = _rms(d_ref[i], onorm_ref[...]) * _silu(z_ref[0, :, i * hd:(i + 1) * hd])
        o_ref[0, :, i * hd:(i + 1) * hd] = gated.astype(BF16)


def _gdn_step_kernel(q_ref, k_ref, v_ref, z_ref, beta_ref, gc_ref, cw_q_ref, cw_k_ref, cw_v_ref,
                     onorm_ref, pq_ref, pk_ref, pv_ref, s0_ref, o_ref, s_ref, *, c, bb, group):
    hk = pl.program_id(1)
    hd = GDN_HEAD

    def per_group(it, carry):
        loaded = []
        for g in range(group):
            bi = it * group + g
            loaded.append((bi, q_ref[bi], k_ref[bi], v_ref[bi], pq_ref[bi], pk_ref[bi], pv_ref[bi],
                           beta_ref[bi], gc_ref[bi], z_ref[bi], s0_ref[bi, 0], s0_ref[bi, 1]))
        chains, meta = [], []
        for (bi, xq, xk, xv, pq, pk, pv, beta, gc, z, s0a, s0b) in loaded:
            q = _l2n(_conv_silu(xq, cw_q_ref, pq)) * (hd ** -0.5)
            k = _l2n(_conv_silu(xk, cw_k_ref, pk))
            vv = _conv_silu(xv, cw_v_ref, pv)
            for i, s0 in enumerate((s0a, s0b)):
                chains.append((q, k, vv[:, i * hd:(i + 1) * hd],
                               _head_column(beta, 2 * hk + i), _head_column(gc, 2 * hk + i)))
                meta.append((bi, i, z[:, i * hd:(i + 1) * hd], s0))
        pre_out = _gdn_chunk_pre(chains, c)
        step_out = _gdn_chunk_step([p + (m[3],) for p, m in zip(pre_out, meta)], c)
        for (bi, i, z, _), (o, s_new) in zip(meta, step_out):
            s_ref[bi, i] = s_new
            gated = _rms(o, onorm_ref[...]) * _silu(z)
            o_ref[bi, :, i * hd:(i + 1) * hd] = gated.astype(BF16)
        return carry

    lax.fori_loop(0, bb // group, per_group, 0)


def gdn_delta(proj, beta, gc, conv_w, o_norm, conv_state, s0):
    b, t, _ = proj.shape
    hv = beta.shape[2]
    hq = hv // 2
    c = min(GDN_CHUNK, t)
    assert t % c == 0
    n_chunks = t // c
    kc = conv_w.shape[0]
    hd = GDN_HEAD
    q0, k0, v0, z0 = 0, hq, hq, 2 * hq
    bb = 1 if s0 is None else _tile(b, 16)

    in_specs = [pl.BlockSpec((bb, t, hd), lambda i, h: (i, 0, q0 + h)),
                pl.BlockSpec((bb, t, hd), lambda i, h: (i, 0, k0 + h)),
                pl.BlockSpec((bb, t, 2 * hd), lambda i, h: (i, 0, v0 + h)),
                pl.BlockSpec((bb, t, 2 * hd), lambda i, h: (i, 0, z0 + h)),
                pl.BlockSpec((bb, t, hv), lambda i, h: (i, 0, 0)),
                pl.BlockSpec((bb, t, hv), lambda i, h: (i, 0, 0)),
                pl.BlockSpec((kc, hd), lambda i, h: (0, q0 + h)),
                pl.BlockSpec((kc, hd), lambda i, h: (0, k0 + h)),
                pl.BlockSpec((kc, 2 * hd), lambda i, h: (0, v0 + h)),
                pl.BlockSpec((1, hd), lambda i, h: (0, 0))]
    args = [proj, proj, proj, proj, beta, gc, conv_w, conv_w, conv_w, o_norm.reshape(1, hd)]
    out_specs = [pl.BlockSpec((bb, t, 2 * hd), lambda i, h: (i, 0, h)),
                 pl.BlockSpec((bb, 2, hd, hd), lambda i, h: (i, h, 0, 0))]
    out_shape = [jax.ShapeDtypeStruct((b, t, hv * hd), BF16),
                 jax.ShapeDtypeStruct((b, hv, hd, hd), F32)]
    if s0 is None:
        group = 8 if n_chunks % 8 == 0 else (4 if n_chunks % 4 == 0 else 1)
        body = functools.partial(_gdn_seq_kernel, t=t, c=c, group=group)
        scratch = [pltpu.VMEM((t, hd), F32), pltpu.VMEM((t, hd), F32), pltpu.VMEM((t, 2 * hd), F32),
                   pltpu.VMEM((2, t, hd), F32), pltpu.VMEM((2, t, hd), F32),
                   pltpu.VMEM((2, n_chunks * (c + hd), hd), BF16),
                   pltpu.VMEM((2, n_chunks * hd, hd), F32),
                   pltpu.VMEM((2, t, hd), F32),
                   pltpu.VMEM((2, n_chunks * SUBLANES, hd), F32),
                   pltpu.VMEM((2, hd, hd), F32)]
    else:
        assert n_chunks == 1
        group = 16 if bb % 16 == 0 else (8 if bb % 8 == 0 else 1)
        body = functools.partial(_gdn_step_kernel, c=c, bb=bb, group=group)
        scratch = []
        in_specs += [pl.BlockSpec((bb, SUBLANES, hd), lambda i, h: (i, 0, q0 + h)),
                     pl.BlockSpec((bb, SUBLANES, hd), lambda i, h: (i, 0, k0 + h)),
                     pl.BlockSpec((bb, SUBLANES, 2 * hd), lambda i, h: (i, 0, v0 + h)),
                     pl.BlockSpec((bb, 2, hd, hd), lambda i, h: (i, h, 0, 0))]
        args += [conv_state, conv_state, conv_state, s0]
    return pl.pallas_call(
        body,
        grid=(b // bb, hq),
        in_specs=in_specs,
        out_specs=out_specs,
        out_shape=out_shape,
        scratch_shapes=scratch,
        compiler_params=_cparams("parallel", "parallel"),
        name="gdn_delta",
    )(*args)


def _conf_conv_kernel(x_ref, halo_ref, w_ref, b_ref, lg_ref, lb_ref, o_ref, xx_ref, *, bb, tm, kw, hpad):
    base = hpad - (kw - 1)
    span = tm + SUBLANES

    def per_batch(bi, carry):
        xx_ref[0:hpad, :] = halo_ref[bi]
        xx_ref[hpad:hpad + tm, :] = x_ref[bi]
        xx_ref[hpad + tm:hpad + span, :] = jnp.zeros((SUBLANES, xx_ref.shape[1]), F32)
        acc = None
        for r in range(SUBLANES):
            z = None
            for j in range(kw):
                if (base + j) % SUBLANES != r:
                    continue
                a = base + j - r
                term = xx_ref[a:a + span, :] * w_ref[j:j + 1, :]
                z = term if z is None else z + term
            if z is None:
                continue
            if r:
                z = pltpu.roll(z, span - r, 0)
            acc = z[:tm] if acc is None else acc + z[:tm]
        acc = acc + b_ref[...]
        mu = jnp.mean(acc, axis=-1, keepdims=True)
        xc = acc - mu
        y = xc * lax.rsqrt(jnp.mean(xc * xc, axis=-1, keepdims=True) + EPS)
        y = y * lg_ref[...] + lb_ref[...]
        o_ref[bi] = _silu(y).astype(BF16)
        return carry

    lax.fori_loop(0, bb, per_batch, 0)


def conf_conv(x, halo, w, b, ln_g, ln_b, bb):
    nb, tm, ch = x.shape
    hpad = halo.shape[1]
    kw = w.shape[0]
    assert nb % bb == 0 and hpad >= kw - 1
    return pl.pallas_call(
        functools.partial(_conf_conv_kernel, bb=bb, tm=tm, kw=kw, hpad=hpad),
        grid=(nb // bb,),
        in_specs=[pl.BlockSpec((bb, tm, ch), lambda i: (i, 0, 0)),
                  pl.BlockSpec((bb, hpad, ch), lambda i: (i, 0, 0)),
                  pl.BlockSpec((kw, ch), lambda i: (0, 0)),
                  pl.BlockSpec((1, ch), lambda i: (0, 0)),
                  pl.BlockSpec((1, ch), lambda i: (0, 0)),
                  pl.BlockSpec((1, ch), lambda i: (0, 0))],
        out_specs=pl.BlockSpec((bb, tm, ch), lambda i: (i, 0, 0)),
        out_shape=jax.ShapeDtypeStruct((nb, tm, ch), BF16),
        scratch_shapes=[pltpu.VMEM((hpad + tm + SUBLANES, ch), F32)],
        compiler_params=_cparams("parallel"),
        name="conf_conv",
    )(x, halo, w, b.reshape(1, ch), ln_g.reshape(1, ch), ln_b.reshape(1, ch))


def _extract_top(s, rounds):
    rows = s.shape[0]
    iota = lax.broadcasted_iota(jnp.int32, s.shape, 0)
    rank = jnp.full(s.shape, float(rounds), F32)
    vals = []
    for r in range(rounds):
        m = jnp.max(s, axis=0, keepdims=True)
        idx = jnp.min(jnp.where(s == m, iota, rows), axis=0, keepdims=True)
        sel = iota == idx
        rank = jnp.where(sel, float(r), rank)
        s = jnp.where(sel, -jnp.inf, s)
        vals.append(m)
    return rank, vals


def _peer_route_kernel(q_ref, k1_ref, k2_ref, r2_ref, e2_ref, lc_ref, co_ref, cand_ref):
    s1 = _dot_nt(k1_ref[...], q_ref[:, :PEER_HALF])
    s2 = _dot_nt(k2_ref[...], q_ref[:, PEER_HALF:])
    rank1, v1 = _extract_top(s1, PEER_TOPK)
    rank2, v2 = _extract_top(s2, PEER_TOPK)
    cand_ref[...] = jnp.full(cand_ref.shape, -jnp.inf, F32)
    for r, (a, b) in enumerate(_CANDS):
        cand_ref[r:r + 1, :] = v1[a] + v2[b]
    cand = cand_ref[...]
    crank, cvals = _extract_top(cand, PEER_TOPK)
    sel = jnp.where(crank < float(PEER_TOPK), 1.0, 0.0)
    zsum = jnp.sum(sel * jnp.exp(cand - cvals[0]), axis=0, keepdims=True)
    crow = lax.broadcasted_iota(jnp.int32, cand.shape, 0)
    lc = jnp.zeros_like(s1)
    pos = 0
    for a in range(PEER_TOPK):
        cnt = sum(1 for (aa, _) in _CANDS if aa == a)
        in_a = (crow >= pos) & (crow < pos + cnt)
        la = jnp.sum(jnp.where(in_a, sel, 0.0), axis=0, keepdims=True)
        lc = lc + jnp.where(rank1 == float(a), la, 0.0)
        pos += cnt
    r2_ref[0] = rank2
    e2_ref[0] = jnp.exp(s2 - v2[0])
    lc_ref[0] = lc
    co_ref[0] = jnp.exp(s1 - v1[0]) / zsum


def peer_route(q, keys1, keys2, tm=512):
    n, qd = q.shape
    tm = _tile(n, tm)
    nk = keys1.shape[0]
    assert qd == PEER_HEADS * 2 * PEER_HALF
    tab = jax.ShapeDtypeStruct((PEER_HEADS, nk, n), F32)
    tspec = pl.BlockSpec((1, nk, tm), lambda i, h: (h, 0, i))
    return pl.pallas_call(
        _peer_route_kernel,
        grid=(n // tm, PEER_HEADS),
        in_specs=[pl.BlockSpec((tm, 2 * PEER_HALF), lambda i, h: (i, h)),
                  pl.BlockSpec(keys1.shape, lambda i, h: (0, 0)),
                  pl.BlockSpec(keys2.shape, lambda i, h: (0, 0))],
        out_specs=[tspec, tspec, tspec, tspec],
        out_shape=[tab, tab, tab, tab],
        scratch_shapes=[pltpu.VMEM((_NCAND_PAD, tm), F32)],
        compiler_params=_cparams("parallel", "parallel"),
        name="peer_route",
    )(q, keys1, keys2)


def _peer_mix_kernel(xn_ref, u_ref, vt_ref, r2_ref, e2_ref, lc_ref, co_ref, o_ref,
                     acc_ref, a_ref, *, eb, sub):
    j = pl.program_id(1)
    nk = PEER_NKEYS
    nsub = eb // sub
    tm = xn_ref.shape[0]
    ct = LANES

    @pl.when(j == 0)
    def _():
        acc_ref[...] = jnp.zeros_like(acc_ref)

    def hidden(s):
        return _dot_nt(u_ref[s * sub:(s + 1) * sub, :], xn_ref[...])

    def gated(s, ht):
        for ii in range(sub // nk):
            i1 = j * (eb // nk) + s * (sub // nk) + ii
            lrows = [lc_ref[h, pl.ds(i1, 1), :] for h in range(PEER_HEADS)]
            crows = [co_ref[h, pl.ds(i1, 1), :] for h in range(PEER_HEADS)]
            for c0 in range(0, tm, ct):
                hv = ht[ii * nk:(ii + 1) * nk, c0:c0 + ct]
                act = 0.5 * hv * (1.0 + lax.erf(hv * (2.0 ** -0.5)))
                gate = None
                for h in range(PEER_HEADS):
                    term = jnp.where(r2_ref[h, :, c0:c0 + ct] < lrows[h][:, c0:c0 + ct],
                                     e2_ref[h, :, c0:c0 + ct] * crows[h][:, c0:c0 + ct], 0.0)
                    gate = term if gate is None else gate + term
                a_ref[s, ii * nk:(ii + 1) * nk, c0:c0 + ct] = (gate * act).astype(BF16)

    hts = [hidden(s) for s in range(nsub)]
    for s in range(nsub):
        gated(s, hts[s])
        acc_ref[...] += jnp.dot(vt_ref[:, s * sub:(s + 1) * sub], a_ref[s], preferred_element_type=F32)

    @pl.when(j == pl.num_programs(1) - 1)
    def _():
        o_ref[...] = acc_ref[...].T


def peer_mix(xn, u, vt, r2, e2, lc, co, tm=512, eb=1024, sub=256):
    n, d = xn.shape
    ne = u.shape[0]
    tm = _tile(n, tm)
    tspec = pl.BlockSpec((PEER_HEADS, PEER_NKEYS, tm), lambda i, j: (0, 0, i))
    return pl.pallas_call(
        functools.partial(_peer_mix_kernel, eb=eb, sub=sub),
        grid=(n // tm, ne // eb),
        in_specs=[pl.BlockSpec((tm, d), lambda i, j: (i, 0)),
                  pl.BlockSpec((eb, d), lambda i, j: (j, 0)),
                  pl.BlockSpec((d, eb), lambda i, j: (0, j)),
                  tspec, tspec, tspec, tspec],
        out_specs=pl.BlockSpec((tm, d), lambda i, j: (i, 0)),
        out_shape=jax.ShapeDtypeStruct((n, d), F32),
        scratch_shapes=[pltpu.VMEM((d, tm), F32), pltpu.VMEM((eb // sub, sub, tm), BF16)],
        compiler_params=_cparams("parallel", "arbitrary"),
        name="peer_mix",
    )(xn, u, vt, r2, e2, lc, co)


def _prep_weights(norm_mix, norm_ffn, norm_ple, final_norm,
                  gdn_w_in, gdn_conv_w, gdn_a_log, gdn_dt_bias, gdn_o_norm, gdn_w_out,
                  conf_w_in, conf_b_in, conf_dw_w, conf_dw_b, conf_ln_g, conf_ln_b, conf_w_out, conf_b_out,
                  peer_w_q, peer_keys1, peer_keys2, peer_u, peer_v,
                  ple_w_proj, ple_w_gate, ple_b_gate):
    hv = gdn_a_log.shape[1]
    main = gdn_w_in.shape[2] - 2 * hv
    return dict(
        gdn_w_main=gdn_w_in[:, :, :main].astype(BF16),
        gdn_w_b=gdn_w_in[:, :, main:main + hv].astype(BF16),
        gdn_w_a=gdn_w_in[:, :, main + hv:].astype(BF16),
        gdn_w_out=gdn_w_out.astype(BF16),
        conf_w_in=conf_w_in.astype(BF16),
        conf_w_out=conf_w_out.astype(BF16),
        peer_w_q=peer_w_q.astype(BF16),
        peer_keys1=peer_keys1.astype(BF16),
        peer_keys2=peer_keys2.astype(BF16),
        peer_u=peer_u.astype(BF16),
        peer_vt=jnp.swapaxes(peer_v.astype(BF16), 1, 2),
        ple_w_proj=ple_w_proj.astype(BF16),
        ple_w_gate=ple_w_gate.astype(BF16),
    )


def _trunk(x, p, gdn_s, gdn_buf, conf_buf, wts, pw):
    (norm_mix, norm_ffn, norm_ple, final_norm,
     gdn_w_in, gdn_conv_w, gdn_a_log, gdn_dt_bias, gdn_o_norm, gdn_w_out,
     conf_w_in, conf_b_in, conf_dw_w, conf_dw_b, conf_ln_g, conf_ln_b, conf_w_out, conf_b_out,
     peer_w_q, peer_keys1, peer_keys2, peer_u, peer_v,
     ple_w_proj, ple_w_gate, ple_b_gate) = wts
    b, t, d = x.shape
    n = b * t
    depth = norm_mix.shape[0]
    has_state = gdn_s is not None
    h = x.reshape(n, d)
    new_s, new_gbuf, new_cbuf = [], [], []
    for i in range(depth):
        j = i // 2
        if i % 2 == 0:
            hv = gdn_a_log.shape[1]
            conv_dim = gdn_conv_w.shape[2]
            kc = gdn_conv_w.shape[1]
            main = pw["gdn_w_main"].shape[2]
            proj = rms_matmul(h, norm_mix[i], pw["gdn_w_main"][j], main)
            chunk = min(GDN_CHUNK, t)
            beta, gc = gdn_gates(h, norm_mix[i], pw["gdn_w_b"][j], pw["gdn_w_a"][j],
                                 gdn_a_log[j], gdn_dt_bias[j], chunk)
            proj3 = proj.reshape(b, t, main)
            if has_state:
                cstate = jnp.pad(gdn_buf[j], ((0, 0), (SUBLANES - (kc - 1), 0), (0, 0)))
                s0 = gdn_s[j]
            else:
                cstate, s0 = None, None
            o, s_fin = gdn_delta(proj3, beta.reshape(b, t, hv), gc.reshape(b, t, hv), gdn_conv_w[j],
                                 gdn_o_norm[j], cstate, s0)
            new_s.append(s_fin)
            new_gbuf.append(proj3[:, t - (kc - 1):, :conv_dim])
            h = matmul_bias_res(o.reshape(n, hv * GDN_HEAD), pw["gdn_w_out"][j],
                                jnp.zeros((d,), F32), h)
        else:
            kw = conf_dw_w.shape[1]
            ch = conf_dw_w.shape[2]
            glu = rms_matmul_glu(h, norm_mix[i], pw["conf_w_in"][j], conf_b_in[j])
            glu3 = glu.reshape(b, t, ch)
            hpad = 32
            if has_state:
                prev = conf_buf[j]
                halo = jnp.pad(prev, ((0, 0), (hpad - (kw - 1), 0), (0, 0)))
                c = conf_conv(glu3, halo, conf_dw_w[j], conf_dw_b[j], conf_ln_g[j], conf_ln_b[j], bb=8)
                new_cbuf.append(jnp.concatenate([prev, glu3], axis=1)[:, t:])
            else:
                tmc = _tile(t, 256)
                nt = t // tmc
                tiles = glu3.reshape(b, nt, tmc, ch)
                halo = jnp.concatenate([jnp.zeros((b, 1, hpad, ch), F32), tiles[:, :-1, tmc - hpad:]], axis=1)
                c = conf_conv(tiles.reshape(b * nt, tmc, ch), halo.reshape(b * nt, hpad, ch),
                              conf_dw_w[j], conf_dw_b[j], conf_ln_g[j], conf_ln_b[j], bb=1)
                new_cbuf.append(glu3[:, t - (kw - 1):])
            h = matmul_bias_res(c.reshape(n, ch), pw["conf_w_out"][j], conf_b_out[j], h)
        q, xn = rms_matmul_xn(h, norm_ffn[i], pw["peer_w_q"][i])
        r2, e2, lc, co = peer_route(q, pw["peer_keys1"][i], pw["peer_keys2"][i])
        y = peer_mix(xn, pw["peer_u"][i], pw["peer_vt"][i], r2, e2, lc, co)
        h = ple_mix(h, y, norm_ple[i], pw["ple_w_gate"][i], ple_b_gate[i],
                    p[i].reshape(n, -1), pw["ple_w_proj"][i])
    out = rmsnorm(h, final_norm).reshape(b, t, d)
    return out, jnp.stack(new_s), jnp.stack(new_gbuf), jnp.stack(new_cbuf)


def kernel(x_prompt, x_sample, p_prompt, p_sample, state_gdn_recurrent, state_gdn_conv, state_conf_conv, norm_mix, norm_ffn, norm_ple, final_norm, gdn_w_in, gdn_conv_w, gdn_a_log, gdn_dt_bias, gdn_o_norm, gdn_w_out, conf_w_in, conf_b_in, conf_dw_w, conf_dw_b, conf_ln_g, conf_ln_b, conf_w_out, conf_b_out, peer_w_q, peer_keys1, peer_keys2, peer_u, peer_v, ple_w_proj, ple_w_gate, ple_b_gate):
    wts = (norm_mix, norm_ffn, norm_ple, final_norm,
           gdn_w_in, gdn_conv_w, gdn_a_log, gdn_dt_bias, gdn_o_norm, gdn_w_out,
           conf_w_in, conf_b_in, conf_dw_w, conf_dw_b, conf_ln_g, conf_ln_b, conf_w_out, conf_b_out,
           peer_w_q, peer_keys1, peer_keys2, peer_u, peer_v,
           ple_w_proj, ple_w_gate, ple_b_gate)
    pw = _prep_weights(*wts)
    y_p, s_p, gb_p, cb_p = _trunk(x_prompt, p_prompt, None, None, None, wts, pw)
    y_s, s_s, gb_s, cb_s = _trunk(x_sample, p_sample, state_gdn_recurrent, state_gdn_conv,
                                  state_conf_conv, wts, pw)
    return (y_p, y_s, s_p, s_s, gb_p, gb_s, cb_p, cb_s)
```

```python
import functools
import math

import jax
import jax.numpy as jnp
from jax import lax
from jax.experimental import pallas as pl
from jax.experimental.pallas import tpu as pltpu

F32 = jnp.float32
BF16 = jnp.bfloat16
EPS = 1e-6

GDN_HEAD = 128
GDN_CHUNK = 64
PEER_HEADS = 8
PEER_NKEYS = 128
PEER_HALF = 128
PEER_TOPK = 16

LANES = 128
SUBLANES = 8
VMEM_LIMIT = 56 * 1024 * 1024

_CANDS = [(a, b) for a in range(PEER_TOPK) for b in range(PEER_TOPK) if (a + 1) * (b + 1) <= PEER_TOPK]
_NCAND_PAD = -(-len(_CANDS) // SUBLANES) * SUBLANES


def _cparams(*sem):
    return pltpu.CompilerParams(dimension_semantics=sem, vmem_limit_bytes=VMEM_LIMIT)


def _tile(n, target):
    t = min(n, target)
    assert n % t == 0, (n, t)
    return t


def _dot(a, b):
    return jnp.dot(a.astype(BF16), b.astype(BF16), preferred_element_type=F32)


def _dot_nt(a, b):
    return lax.dot_general(a.astype(BF16), b.astype(BF16), (((1,), (1,)), ((), ())),
                           preferred_element_type=F32)


def _rms(x, g):
    return x * lax.rsqrt(jnp.mean(x * x, axis=-1, keepdims=True) + EPS) * g


def _sigmoid(x):
    return 1.0 / (1.0 + jnp.exp(-x))


def _silu(x):
    return x * _sigmoid(x)


def _rms_mm_kernel(x_ref, g_ref, w_ref, o_ref, xn_ref):
    @pl.when(pl.program_id(1) == 0)
    def _():
        xn_ref[...] = _rms(x_ref[...], g_ref[...]).astype(BF16)

    o_ref[...] = jnp.dot(xn_ref[...], w_ref[...], preferred_element_type=F32)


def rms_matmul(x, g, w, n_cols, tn=512, tm=1024):
    n, d = x.shape
    tm = _tile(n, tm)
    return pl.pallas_call(
        _rms_mm_kernel,
        grid=(n // tm, n_cols // tn),
        in_specs=[pl.BlockSpec((tm, d), lambda i, j: (i, 0)),
                  pl.BlockSpec((1, d), lambda i, j: (0, 0)),
                  pl.BlockSpec((d, tn), lambda i, j: (0, j))],
        out_specs=pl.BlockSpec((tm, tn), lambda i, j: (i, j)),
        out_shape=jax.ShapeDtypeStruct((n, n_cols), F32),
        scratch_shapes=[pltpu.VMEM((tm, d), BF16)],
        compiler_params=_cparams("parallel", "arbitrary"),
        name="rms_matmul",
    )(x, g.reshape(1, d), w)


def rms_matmul_xn(x, g, w, tn=512, tm=1024):
    n, d = x.shape
    n_cols = w.shape[1]
    tm = _tile(n, tm)
    return pl.pallas_call(
        _rms_mm_kernel,
        grid=(n // tm, n_cols // tn),
        in_specs=[pl.BlockSpec((tm, d), lambda i, j: (i, 0)),
                  pl.BlockSpec((1, d), lambda i, j: (0, 0)),
                  pl.BlockSpec((d, tn), lambda i, j: (0, j))],
        out_specs=[pl.BlockSpec((tm, tn), lambda i, j: (i, j)),
                   pl.BlockSpec((tm, d), lambda i, j: (i, 0))],
        out_shape=[jax.ShapeDtypeStruct((n, n_cols), F32),
                   jax.ShapeDtypeStruct((n, d), BF16)],
        compiler_params=_cparams("parallel", "arbitrary"),
        name="rms_matmul_xn",
    )(x, g.reshape(1, d), w)


def _mm_res_kernel(x_ref, w_ref, b_ref, r_ref, o_ref):
    o_ref[...] = (jnp.dot(x_ref[...], w_ref[...], preferred_element_type=F32)
                  + b_ref[...] + r_ref[...])


def matmul_bias_res(x, w, b, res, tn=512, tm=1024):
    n, k = x.shape
    n_cols = w.shape[1]
    tm = _tile(n, tm)
    return pl.pallas_call(
        _mm_res_kernel,
        grid=(n // tm, n_cols // tn),
        in_specs=[pl.BlockSpec((tm, k), lambda i, j: (i, 0)),
                  pl.BlockSpec((k, tn), lambda i, j: (0, j)),
                  pl.BlockSpec((1, tn), lambda i, j: (0, j)),
                  pl.BlockSpec((tm, tn), lambda i, j: (i, j))],
        out_specs=pl.BlockSpec((tm, tn), lambda i, j: (i, j)),
        out_shape=jax.ShapeDtypeStruct((n, n_cols), F32),
        compiler_params=_cparams("parallel", "arbitrary"),
        name="matmul_bias_res",
    )(x, w, b.reshape(1, n_cols), res)


def _glu_kernel(x_ref, g_ref, wa_ref, wb_ref, ba_ref, bb_ref, o_ref, xn_ref):
    @pl.when(pl.program_id(1) == 0)
    def _():
        xn_ref[...] = _rms(x_ref[...], g_ref[...]).astype(BF16)

    a = jnp.dot(xn_ref[...], wa_ref[...], preferred_element_type=F32) + ba_ref[...]
    b = jnp.dot(xn_ref[...], wb_ref[...], preferred_element_type=F32) + bb_ref[...]
    o_ref[...] = a * _sigmoid(b)


def rms_matmul_glu(x, g, w, b, tn=512, tm=1024):
    n, d = x.shape
    c = w.shape[1] // 2
    tm = _tile(n, tm)
    nj = c // tn
    b2 = b.reshape(1, 2 * c)
    return pl.pallas_call(
        _glu_kernel,
        grid=(n // tm, nj),
        in_specs=[pl.BlockSpec((tm, d), lambda i, j: (i, 0)),
                  pl.BlockSpec((1, d), lambda i, j: (0, 0)),
                  pl.BlockSpec((d, tn), lambda i, j: (0, j)),
                  pl.BlockSpec((d, tn), lambda i, j: (0, j + nj)),
                  pl.BlockSpec((1, tn), lambda i, j: (0, j)),
                  pl.BlockSpec((1, tn), lambda i, j: (0, j + nj))],
        out_specs=pl.BlockSpec((tm, tn), lambda i, j: (i, j)),
        out_shape=jax.ShapeDtypeStruct((n, c), F32),
        scratch_shapes=[pltpu.VMEM((tm, d), BF16)],
        compiler_params=_cparams("parallel", "arbitrary"),
        name="rms_matmul_glu",
    )(x, g.reshape(1, d), w, w, b2, b2)


def _ple_kernel(h_ref, y_ref, g_ref, wg_ref, bg_ref, p_ref, wp_ref, o_ref, xn_ref, hh_ref):
    j = pl.program_id(1)
    tn = o_ref.shape[1]

    @pl.when(j == 0)
    def _():
        hh = h_ref[...] + y_ref[...]
        xn_ref[...] = _rms(hh, g_ref[...]).astype(BF16)
        for jj in range(hh_ref.shape[0]):
            hh_ref[jj] = hh[:, jj * tn:(jj + 1) * tn]

    gate = _sigmoid(jnp.dot(xn_ref[...], wg_ref[...], preferred_element_type=F32) + bg_ref[...])
    proj = jnp.dot(p_ref[...].astype(BF16), wp_ref[...], preferred_element_type=F32)
    o_ref[...] = hh_ref[j] + proj * gate


def ple_mix(h, y, g, w_gate, b_gate, p, w_proj, tn=512, tm=512):
    n, d = h.shape
    pd = p.shape[1]
    tm = _tile(n, tm)
    return pl.pallas_call(
        _ple_kernel,
        grid=(n // tm, d // tn),
        in_specs=[pl.BlockSpec((tm, d), lambda i, j: (i, 0)),
                  pl.BlockSpec((tm, d), lambda i, j: (i, 0)),
                  pl.BlockSpec((1, d), lambda i, j: (0, 0)),
                  pl.BlockSpec((d, tn), lambda i, j: (0, j)),
                  pl.BlockSpec((1, tn), lambda i, j: (0, j)),
                  pl.BlockSpec((tm, pd), lambda i, j: (i, 0)),
                  pl.BlockSpec((pd, tn), lambda i, j: (0, j))],
        out_specs=pl.BlockSpec((tm, tn), lambda i, j: (i, j)),
        out_shape=jax.ShapeDtypeStruct((n, d), F32),
        scratch_shapes=[pltpu.VMEM((tm, d), BF16), pltpu.VMEM((d // tn, tm, tn), F32)],
        compiler_params=_cparams("parallel", "arbitrary"),
        name="ple_mix",
    )(h, y, g.reshape(1, d), w_gate, b_gate.reshape(1, d), p, w_proj)


def _rmsnorm_kernel(x_ref, g_ref, o_ref):
    o_ref[...] = _rms(x_ref[...], g_ref[...])


def rmsnorm(x, g, tm=512):
    n, d = x.shape
    tm = _tile(n, tm)
    return pl.pallas_call(
        _rmsnorm_kernel,
        grid=(n // tm,),
        in_specs=[pl.BlockSpec((tm, d), lambda i: (i, 0)),
                  pl.BlockSpec((1, d), lambda i: (0, 0))],
        out_specs=pl.BlockSpec((tm, d), lambda i: (i, 0)),
        out_shape=jax.ShapeDtypeStruct((n, d), F32),
        compiler_params=_cparams("parallel"),
        name="final_rmsnorm",
    )(x, g.reshape(1, d))


def _gdn_gates_kernel(x_ref, g_ref, wb_ref, wa_ref, alog_ref, dtb_ref, beta_ref, gc_ref, *, chunk):
    xn = _rms(x_ref[...], g_ref[...]).astype(BF16)
    b_raw = jnp.dot(xn, wb_ref[...], preferred_element_type=F32)
    a_raw = jnp.dot(xn, wa_ref[...], preferred_element_type=F32)
    beta_ref[...] = _sigmoid(b_raw)
    z = a_raw + dtb_ref[...]
    softplus = jnp.maximum(z, 0.0) + jnp.log(1.0 + jnp.exp(-jnp.abs(z)))
    gc = -jnp.exp(alog_ref[...]) * softplus
    pos = lax.broadcasted_iota(jnp.int32, gc.shape, 0) % chunk
    s = 1
    while s < chunk:
        gc = gc + jnp.where(pos >= s, pltpu.roll(gc, s, 0), 0.0)
        s *= 2
    gc_ref[...] = gc


def gdn_gates(x, g, w_b, w_a, a_log, dt_bias, chunk, tm=512):
    n, d = x.shape
    hv = w_b.shape[1]
    tm = _tile(n, tm)
    assert tm % chunk == 0
    return pl.pallas_call(
        functools.partial(_gdn_gates_kernel, chunk=chunk),
        grid=(n // tm,),
        in_specs=[pl.BlockSpec((tm, d), lambda i: (i, 0)),
                  pl.BlockSpec((1, d), lambda i: (0, 0)),
                  pl.BlockSpec((d, hv), lambda i: (0, 0)),
                  pl.BlockSpec((d, hv), lambda i: (0, 0)),
                  pl.BlockSpec((1, hv), lambda i: (0, 0)),
                  pl.BlockSpec((1, hv), lambda i: (0, 0))],
        out_specs=[pl.BlockSpec((tm, hv), lambda i: (i, 0)),
                   pl.BlockSpec((tm, hv), lambda i: (i, 0))],
        out_shape=[jax.ShapeDtypeStruct((n, hv), F32),
                   jax.ShapeDtypeStruct((n, hv), F32)],
        compiler_params=_cparams("parallel"),
        name="gdn_gates",
    )(x, g.reshape(1, d), w_b, w_a, a_log.reshape(1, hv), dt_bias.reshape(1, hv))


def _conv_silu(x, w_ref, prev):
    kconv = w_ref.shape[0]
    tpos = lax.broadcasted_iota(jnp.int32, x.shape, 0)
    y = x * w_ref[kconv - 1:kconv, :]
    for s in range(1, kconv):
        tail = 0.0 if prev is None else pltpu.roll(prev, s, 0)
        shifted = jnp.where(tpos >= s, pltpu.roll(x, s, 0), tail)
        y = y + shifted * w_ref[kconv - 1 - s:kconv - s, :]
    return _silu(y)


def _l2n(x):
    return x * lax.rsqrt(jnp.sum(x * x, axis=-1, keepdims=True) + EPS)


def _head_column(x, head):
    lane = lax.broadcasted_iota(jnp.int32, x.shape, 1)
    col = jnp.sum(jnp.where(lane == head, x, 0.0), axis=1, keepdims=True)
    return jnp.broadcast_to(col, (x.shape[0], GDN_HEAD))


def _gdn_chunk_pre(chains, c, between=None):
    def tick():
        if between is not None:
            between()

    row = lax.broadcasted_iota(jnp.int32, (c, c), 0)
    col = lax.broadcasted_iota(jnp.int32, (c, c), 1)
    eye = jnp.where(row == col, 1.0, 0.0)
    st = []
    for (q, k, v, bet, gcb) in chains:
        gi = gcb[:, :c]
        gj = gcb.T[:c, :]
        decay = jnp.exp(jnp.where(row >= col, gi - gj, -jnp.inf))
        egc = jnp.exp(gcb)
        kb = k * bet
        st.append(dict(q=q, k=k, gcb=gcb, decay=decay, egc=egc, kb=kb, vb=v * bet))
    for s in st:
        s["kq"] = _dot_nt(jnp.concatenate([s["kb"], s["q"]], axis=0), s["k"])
    tick()
    for s in st:
        s["attn"] = s["kq"][c:] * s["decay"]
        s["p"] = -jnp.where(row > col, s["kq"][:c] * s["decay"], 0.0)
        s["x"] = eye + s["p"]
    for s in st:
        s["p"] = _dot(s["p"], s["p"])
    tick()
    for _ in range(int(math.log2(c)) - 2):
        for s in st:
            s["m"] = _dot(jnp.concatenate([s["p"], s["x"]], axis=0), s["p"])
        tick()
        for s in st:
            s["p"] = s["m"][:c]
            s["x"] = s["x"] + s["m"][c:]
    for s in st:
        s["m"] = _dot(s["x"], s["p"])
    tick()
    for s in st:
        s["x"] = s["x"] + s["m"]
        s["rhs"] = jnp.concatenate([s["vb"], s["kb"] * s["egc"]], axis=1)
    for s in st:
        s["uw"] = _dot(s["x"], s["rhs"])
    tick()
    for s in st:
        s["glast"] = s["gcb"][c - 1:c, :]
        kdect = (s["k"] * jnp.exp(s["glast"] - s["gcb"])).T
        s["t"] = _dot(jnp.concatenate([kdect, s["attn"]], axis=0), s["uw"])
    tick()
    out = []
    hd = GDN_HEAD
    for s in st:
        t = s["t"]
        m = jnp.concatenate([-t[:hd, hd:], s["q"] * s["egc"] - t[hd:, hd:]], axis=0)
        out.append((m, t[:hd, :hd], t[hd:, :hd], jnp.exp(s["glast"])))
    return out


def _gdn_chunk_step(chains, c):
    rs = [_dot(m, s_prev) for (m, b, d, eg, s_prev) in chains]
    return [(r[GDN_HEAD:] + d, s_prev * eg + r[:GDN_HEAD] + b)
            for (m, b, d, eg, s_prev), r in zip(chains, rs)]


def _gdn_seq_kernel(q_ref, k_ref, v_ref, z_ref, beta_ref, gc_ref, cw_q_ref, cw_k_ref, cw_v_ref,
                    onorm_ref, o_ref, s_ref,
                    qs_ref, ks_ref, vs_ref, bs_ref, gs_ref, m_ref, b_ref, d_ref, eg_ref, st_ref,
                    *, t, c, group):
    hk = pl.program_id(1)
    n_chunks = t // c
    hd = GDN_HEAD
    qs_ref[...] = _l2n(_conv_silu(q_ref[0], cw_q_ref, None)) * (hd ** -0.5)
    ks_ref[...] = _l2n(_conv_silu(k_ref[0], cw_k_ref, None))
    vs_ref[...] = _conv_silu(v_ref[0], cw_v_ref, None)
    for i in range(2):
        bs_ref[i] = _head_column(beta_ref[0], 2 * hk + i)
        gs_ref[i] = _head_column(gc_ref[0], 2 * hk + i)
        st_ref[i] = jnp.zeros((hd, hd), F32)

    def rows(n, size):
        start = n * size
        return pl.ds(start if isinstance(start, int) else pl.multiple_of(start, size), size)

    def pre_group(it, between):
        loaded = []
        for g in range(group):
            n = it * group + g
            q = qs_ref[rows(n, c), :]
            k = ks_ref[rows(n, c), :]
            for i in range(2):
                loaded.append((n, i, q, k, vs_ref[rows(n, c), i * hd:(i + 1) * hd],
                               bs_ref[i, rows(n, c), :], gs_ref[i, rows(n, c), :]))
        pre_out = _gdn_chunk_pre([x[2:] for x in loaded], c, between)
        for (n, i, *_), (m, b, d, eg) in zip(loaded, pre_out):
            m_ref[i, rows(n, c + hd), :] = m.astype(BF16)
            b_ref[i, rows(n, hd), :] = b
            d_ref[i, rows(n, c), :] = d
            eg_ref[i, rows(n, SUBLANES), :] = jnp.broadcast_to(eg, (SUBLANES, hd))

    def step_chunk(n):
        chains = []
        for i in range(2):
            chains.append((m_ref[i, rows(n, c + hd), :], b_ref[i, rows(n, hd), :],
                           d_ref[i, rows(n, c), :], eg_ref[i, rows(n, SUBLANES), :][0:1],
                           st_ref[i]))
        for i, (o, s_new) in enumerate(_gdn_chunk_step(chains, c)):
            st_ref[i] = s_new
            d_ref[i, rows(n, c), :] = o

    n_groups = n_chunks // group
    pre_group(0, None)

    def body(it, carry):
        pending = list(range(group))

        def between():
            if pending:
                step_chunk((it - 1) * group + pending.pop(0))

        pre_group(it, between)
        while pending:
            step_chunk((it - 1) * group + pending.pop(0))
        return carry

    if n_groups <= 4:
        for it in range(1, n_groups):
            body(it, 0)
    else:
        lax.fori_loop(1, n_groups, body, 0)
    for g in range(group):
        step_chunk((n_groups - 1) * group + g)

    for i in range(2):
        s_ref[0, i] = st_ref[i]
        gated = _rms(d_ref[i], onorm_ref[...]) * _silu(z_ref[0, :, i * hd:(i + 1) * hd])
        o_ref[0, :, i * hd:(i + 1) * hd] = gated.astype(BF16)


def _gdn_step_kernel(q_ref, k_ref, v_ref, z_ref, beta_ref, gc_ref, cw_q_ref, cw_k_ref, cw_v_ref,
                     onorm_ref, pq_ref, pk_ref, pv_ref, s0_ref, o_ref, s_ref, *, c, bb, group):
    hk = pl.program_id(1)
    hd = GDN_HEAD

    def per_group(it, carry):
        loaded = []
        for g in range(group):
            bi = it * group + g
            loaded.append((bi, q_ref[bi], k_ref[bi], v_ref[bi], pq_ref[bi], pk_ref[bi], pv_ref[bi],
                           beta_ref[bi], gc_ref[bi], z_ref[bi], s0_ref[bi, 0], s0_ref[bi, 1]))
        chains, meta = [], []
        for (bi, xq, xk, xv, pq, pk, pv, beta, gc, z, s0a, s0b) in loaded:
            q = _l2n(_conv_silu(xq, cw_q_ref, pq)) * (hd ** -0.5)
            k = _l2n(_conv_silu(xk, cw_k_ref, pk))
            vv = _conv_silu(xv, cw_v_ref, pv)
            for i, s0 in enumerate((s0a, s0b)):
                chains.append((q, k, vv[:, i * hd:(i + 1) * hd],
                               _head_column(beta, 2 * hk + i), _head_column(gc, 2 * hk + i)))
                meta.append((bi, i, z[:, i * hd:(i + 1) * hd], s0))
        pre_out = _gdn_chunk_pre(chains, c)
        step_out = _gdn_chunk_step([p + (m[3],) for p, m in zip(pre_out, meta)], c)
        for (bi, i, z, _), (o, s_new) in zip(meta, step_out):
            s_ref[bi, i] = s_new
            gated = _rms(o, onorm_ref[...]) * _silu(z)
            o_ref[bi, :, i * hd:(i + 1) * hd] = gated.astype(BF16)
        return carry

    lax.fori_loop(0, bb // group, per_group, 0)


def gdn_delta(proj, beta, gc, conv_w, o_norm, conv_state, s0):
    b, t, _ = proj.shape
    hv = beta.shape[2]
    hq = hv // 2
    c = min(GDN_CHUNK, t)
    assert t % c == 0
    n_chunks = t // c
    kc = conv_w.shape[0]
    hd = GDN_HEAD
    q0, k0, v0, z0 = 0, hq, hq, 2 * hq
    bb = 1 if s0 is None else _tile(b, 16)

    in_specs = [pl.BlockSpec((bb, t, hd), lambda i, h: (i, 0, q0 + h)),
                pl.BlockSpec((bb, t, hd), lambda i, h: (i, 0, k0 + h)),
                pl.BlockSpec((bb, t, 2 * hd), lambda i, h: (i, 0, v0 + h)),
                pl.BlockSpec((bb, t, 2 * hd), lambda i, h: (i, 0, z0 + h)),
                pl.BlockSpec((bb, t, hv), lambda i, h: (i, 0, 0)),
                pl.BlockSpec((bb, t, hv), lambda i, h: (i, 0, 0)),
                pl.BlockSpec((kc, hd), lambda i, h: (0, q0 + h)),
                pl.BlockSpec((kc, hd), lambda i, h: (0, k0 + h)),
                pl.BlockSpec((kc, 2 * hd), lambda i, h: (0, v0 + h)),
                pl.BlockSpec((1, hd), lambda i, h: (0, 0))]
    args = [proj, proj, proj, proj, beta, gc, conv_w, conv_w, conv_w, o_norm.reshape(1, hd)]
    out_specs = [pl.BlockSpec((bb, t, 2 * hd), lambda i, h: (i, 0, h)),
                 pl.BlockSpec((bb, 2, hd, hd), lambda i, h: (i, h, 0, 0))]
    out_shape = [jax.ShapeDtypeStruct((b, t, hv * hd), BF16),
                 jax.ShapeDtypeStruct((b, hv, hd, hd), F32)]
    if s0 is None:
        group = 8 if n_chunks % 8 == 0 else (4 if n_chunks % 4 == 0 else 1)
        body = functools.partial(_gdn_seq_kernel, t=t, c=c, group=group)
        scratch = [pltpu.VMEM((t, hd), F32), pltpu.VMEM((t, hd), F32), pltpu.VMEM((t, 2 * hd), F32),
                   pltpu.VMEM((2, t, hd), F32), pltpu.VMEM((2, t, hd), F32),
                   pltpu.VMEM((2, n_chunks * (c + hd), hd), BF16),
                   pltpu.VMEM((2, n_chunks * hd, hd), F32),
                   pltpu.VMEM((2, t, hd), F32),
                   pltpu.VMEM((2, n_chunks * SUBLANES, hd), F32),
                   pltpu.VMEM((2, hd, hd), F32)]
    else:
        assert n_chunks == 1
        group = 16 if bb % 16 == 0 else (8 if bb % 8 == 0 else 1)
        body = functools.partial(_gdn_step_kernel, c=c, bb=bb, group=group)
        scratch = []
        in_specs += [pl.BlockSpec((bb, SUBLANES, hd), lambda i, h: (i, 0, q0 + h)),
                     pl.BlockSpec((bb, SUBLANES, hd), lambda i, h: (i, 0, k0 + h)),
                     pl.BlockSpec((bb, SUBLANES, 2 * hd), lambda i, h: (i, 0, v0 + h)),
                     pl.BlockSpec((bb, 2, hd, hd), lambda i, h: (i, h, 0, 0))]
        args += [conv_state, conv_state, conv_state, s0]
    return pl.pallas_call(
        body,
        grid=(b // bb, hq),
        in_specs=in_specs,
        out_specs=out_specs,
        out_shape=out_shape,
        scratch_shapes=scratch,
        compiler_params=_cparams("parallel", "parallel"),
        name="gdn_delta",
    )(*args)


def _conf_conv_kernel(x_ref, halo_ref, w_ref, b_ref, lg_ref, lb_ref, o_ref, xx_ref, *, bb, tm, kw, hpad):
    base = hpad - (kw - 1)
    span = tm + SUBLANES

    def per_batch(bi, carry):
        xx_ref[0:hpad, :] = halo_ref[bi]
        xx_ref[hpad:hpad + tm, :] = x_ref[bi]
        xx_ref[hpad + tm:hpad + span, :] = jnp.zeros((SUBLANES, xx_ref.shape[1]), F32)
        acc = None
        for r in range(SUBLANES):
            z = None
            for j in range(kw):
                if (base + j) % SUBLANES != r:
                    continue
                a = base + j - r
                term = xx_ref[a:a + span, :] * w_ref[j:j + 1, :]
                z = term if z is None else z + term
            if z is None:
                continue
            if r:
                z = pltpu.roll(z, span - r, 0)
            acc = z[:tm] if acc is None else acc + z[:tm]
        acc = acc + b_ref[...]
        mu = jnp.mean(acc, axis=-1, keepdims=True)
        xc = acc - mu
        y = xc * lax.rsqrt(jnp.mean(xc * xc, axis=-1, keepdims=True) + EPS)
        y = y * lg_ref[...] + lb_ref[...]
        o_ref[bi] = _silu(y).astype(BF16)
        return carry

    lax.fori_loop(0, bb, per_batch, 0)


def conf_conv(x, halo, w, b, ln_g, ln_b, bb):
    nb, tm, ch = x.shape
    hpad = halo.shape[1]
    kw = w.shape[0]
    assert nb % bb == 0 and hpad >= kw - 1
    return pl.pallas_call(
        functools.partial(_conf_conv_kernel, bb=bb, tm=tm, kw=kw, hpad=hpad),
        grid=(nb // bb,),
        in_specs=[pl.BlockSpec((bb, tm, ch), lambda i: (i, 0, 0)),
                  pl.BlockSpec((bb, hpad, ch), lambda i: (i, 0, 0)),
                  pl.BlockSpec((kw, ch), lambda i: (0, 0)),
                  pl.BlockSpec((1, ch), lambda i: (0, 0)),
                  pl.BlockSpec((1, ch), lambda i: (0, 0)),
                  pl.BlockSpec((1, ch), lambda i: (0, 0))],
        out_specs=pl.BlockSpec((bb, tm, ch), lambda i: (i, 0, 0)),
        out_shape=jax.ShapeDtypeStruct((nb, tm, ch), BF16),
        scratch_shapes=[pltpu.VMEM((hpad + tm + SUBLANES, ch), F32)],
        compiler_params=_cparams("parallel"),
        name="conf_conv",
    )(x, halo, w, b.reshape(1, ch), ln_g.reshape(1, ch), ln_b.reshape(1, ch))


def _extract_top(s, rounds):
    rows = s.shape[0]
    iota = lax.broadcasted_iota(jnp.int32, s.shape, 0)
    rank = jnp.full(s.shape, float(rounds), F32)
    vals = []
    for r in range(rounds):
        m = jnp.max(s, axis=0, keepdims=True)
        idx = jnp.min(jnp.where(s == m, iota, rows), axis=0, keepdims=True)
        sel = iota == idx
        rank = jnp.where(sel, float(r), rank)
        s = jnp.where(sel, -jnp.inf, s)
        vals.append(m)
    return rank, vals


def _peer_route_kernel(q_ref, k1_ref, k2_ref, r2_ref, e2_ref, lc_ref, co_ref, cand_ref):
    s1 = _dot_nt(k1_ref[...], q_ref[:, :PEER_HALF])
    s2 = _dot_nt(k2_ref[...], q_ref[:, PEER_HALF:])
    rank1, v1 = _extract_top(s1, PEER_TOPK)
    rank2, v2 = _extract_top(s2, PEER_TOPK)
    cand_ref[...] = jnp.full(cand_ref.shape, -jnp.inf, F32)
    for r, (a, b) in enumerate(_CANDS):
        cand_ref[r:r + 1, :] = v1[a] + v2[b]
    cand = cand_ref[...]
    crank, cvals = _extract_top(cand, PEER_TOPK)
    sel = jnp.where(crank < float(PEER_TOPK), 1.0, 0.0)
    zsum = jnp.sum(sel * jnp.exp(cand - cvals[0]), axis=0, keepdims=True)
    crow = lax.broadcasted_iota(jnp.int32, cand.shape, 0)
    lc = jnp.zeros_like(s1)
    pos = 0
    for a in range(PEER_TOPK):
        cnt = sum(1 for (aa, _) in _CANDS if aa == a)
        in_a = (crow >= pos) & (crow < pos + cnt)
        la = jnp.sum(jnp.where(in_a, sel, 0.0), axis=0, keepdims=True)
        lc = lc + jnp.where(rank1 == float(a), la, 0.0)
        pos += cnt
    r2_ref[0] = rank2
    e2_ref[0] = jnp.exp(s2 - v2[0])
    lc_ref[0] = lc
    co_ref[0] = jnp.exp(s1 - v1[0]) / zsum


def peer_route(q, keys1, keys2, tm=512):
    n, qd = q.shape
    tm = _tile(n, tm)
    nk = keys1.shape[0]
    assert qd == PEER_HEADS * 2 * PEER_HALF
    tab = jax.ShapeDtypeStruct((PEER_HEADS, nk, n), F32)
    tspec = pl.BlockSpec((1, nk, tm), lambda i, h: (h, 0, i))
    return pl.pallas_call(
        _peer_route_kernel,
        grid=(n // tm, PEER_HEADS),
        in_specs=[pl.BlockSpec((tm, 2 * PEER_HALF), lambda i, h: (i, h)),
                  pl.BlockSpec(keys1.shape, lambda i, h: (0, 0)),
                  pl.BlockSpec(keys2.shape, lambda i, h: (0, 0))],
        out_specs=[tspec, tspec, tspec, tspec],
        out_shape=[tab, tab, tab, tab],
        scratch_shapes=[pltpu.VMEM((_NCAND_PAD, tm), F32)],
        compiler_params=_cparams("parallel", "parallel"),
        name="peer_route",
    )(q, keys1, keys2)


def _peer_mix_kernel(xn_ref, u_ref, vt_ref, r2_ref, e2_ref, lc_ref, co_ref, o_ref,
                     acc_ref, a_ref, *, eb, sub):
    j = pl.program_id(1)
    nk = PEER_NKEYS
    nsub = eb // sub
    tm = xn_ref.shape[0]
    ct = LANES

    @pl.when(j == 0)
    def _():
        acc_ref[...] = jnp.zeros_like(acc_ref)

    def hidden(s):
        return _dot_nt(u_ref[s * sub:(s + 1) * sub, :], xn_ref[...])

    def gated(s, ht):
        for ii in range(sub // nk):
            i1 = j * (eb // nk) + s * (sub // nk) + ii
            lrows = [lc_ref[h, pl.ds(i1, 1), :] for h in range(PEER_HEADS)]
            crows = [co_ref[h, pl.ds(i1, 1), :] for h in range(PEER_HEADS)]
            for c0 in range(0, tm, ct):
                hv = ht[ii * nk:(ii + 1) * nk, c0:c0 + ct]
                act = 0.5 * hv * (1.0 + lax.erf(hv * (2.0 ** -0.5)))
                gate = None
                for h in range(PEER_HEADS):
                    term = jnp.where(r2_ref[h, :, c0:c0 + ct] < lrows[h][:, c0:c0 + ct],
                                     e2_ref[h, :, c0:c0 + ct] * crows[h][:, c0:c0 + ct], 0.0)
                    gate = term if gate is None else gate + term
                a_ref[s, ii * nk:(ii + 1) * nk, c0:c0 + ct] = (gate * act).astype(BF16)

    hts = [hidden(s) for s in range(nsub)]
    for s in range(nsub):
        gated(s, hts[s])
        acc_ref[...] += jnp.dot(vt_ref[:, s * sub:(s + 1) * sub], a_ref[s], preferred_element_type=F32)

    @pl.when(j == pl.num_programs(1) - 1)
    def _():
        o_ref[...] = acc_ref[...].T


def peer_mix(xn, u, vt, r2, e2, lc, co, tm=512, eb=1024, sub=256):
    n, d = xn.shape
    ne = u.shape[0]
    tm = _tile(n, tm)
    tspec = pl.BlockSpec((PEER_HEADS, PEER_NKEYS, tm), lambda i, j: (0, 0, i))
    return pl.pallas_call(
        functools.partial(_peer_mix_kernel, eb=eb, sub=sub),
        grid=(n // tm, ne // eb),
        in_specs=[pl.BlockSpec((tm, d), lambda i, j: (i, 0)),
                  pl.BlockSpec((eb, d), lambda i, j: (j, 0)),
                  pl.BlockSpec((d, eb), lambda i, j: (0, j)),
                  tspec, tspec, tspec, tspec],
        out_specs=pl.BlockSpec((tm, d), lambda i, j: (i, 0)),
        out_shape=jax.ShapeDtypeStruct((n, d), F32),
        scratch_shapes=[pltpu.VMEM((d, tm), F32), pltpu.VMEM((eb // sub, sub, tm), BF16)],
        compiler_params=_cparams("parallel", "arbitrary"),
        name="peer_mix",
    )(xn, u, vt, r2, e2, lc, co)


def _prep_weights(norm_mix, norm_ffn, norm_ple, final_norm,
                  gdn_w_in, gdn_conv_w, gdn_a_log, gdn_dt_bias, gdn_o_norm, gdn_w_out,
                  conf_w_in, conf_b_in, conf_dw_w, conf_dw_b, conf_ln_g, conf_ln_b, conf_w_out, conf_b_out,
                  peer_w_q, peer_keys1, peer_keys2, peer_u, peer_v,
                  ple_w_proj, ple_w_gate, ple_b_gate):
    hv = gdn_a_log.shape[1]
    main = gdn_w_in.shape[2] - 2 * hv
    return dict(
        gdn_w_main=gdn_w_in[:, :, :main].astype(BF16),
        gdn_w_b=gdn_w_in[:, :, main:main + hv].astype(BF16),
        gdn_w_a=gdn_w_in[:, :, main + hv:].astype(BF16),
        gdn_w_out=gdn_w_out.astype(BF16),
        conf_w_in=conf_w_in.astype(BF16),
        conf_w_out=conf_w_out.astype(BF16),
        peer_w_q=peer_w_q.astype(BF16),
        peer_keys1=peer_keys1.astype(BF16),
        peer_keys2=peer_keys2.astype(BF16),
        peer_u=peer_u.astype(BF16),
        peer_vt=jnp.swapaxes(peer_v.astype(BF16), 1, 2),
        ple_w_proj=ple_w_proj.astype(BF16),
        ple_w_gate=ple_w_gate.astype(BF16),
    )


def _trunk(x, p, gdn_s, gdn_buf, conf_buf, wts, pw):
    (norm_mix, norm_ffn, norm_ple, final_norm,
     gdn_w_in, gdn_conv_w, gdn_a_log, gdn_dt_bias, gdn_o_norm, gdn_w_out,
     conf_w_in, conf_b_in, conf_dw_w, conf_dw_b, conf_ln_g, conf_ln_b, conf_w_out, conf_b_out,
     peer_w_q, peer_keys1, peer_keys2, peer_u, peer_v,
     ple_w_proj, ple_w_gate, ple_b_gate) = wts
    b, t, d = x.shape
    n = b * t
    depth = norm_mix.shape[0]
    has_state = gdn_s is not None
    h = x.reshape(n, d)
    new_s, new_gbuf, new_cbuf = [], [], []
    for i in range(depth):
        j = i // 2
        if i % 2 == 0:
            hv = gdn_a_log.shape[1]
            conv_dim = gdn_conv_w.shape[2]
            kc = gdn_conv_w.shape[1]
            main = pw["gdn_w_main"].shape[2]
            proj = rms_matmul(h, norm_mix[i], pw["gdn_w_main"][j], main, tn=1024)
            chunk = min(GDN_CHUNK, t)
            beta, gc = gdn_gates(h, norm_mix[i], pw["gdn_w_b"][j], pw["gdn_w_a"][j],
                                 gdn_a_log[j], gdn_dt_bias[j], chunk)
            proj3 = proj.reshape(b, t, main)
            if has_state:
                cstate = jnp.pad(gdn_buf[j], ((0, 0), (SUBLANES - (kc - 1), 0), (0, 0)))
                s0 = gdn_s[j]
            else:
                cstate, s0 = None, None
            o, s_fin = gdn_delta(proj3, beta.reshape(b, t, hv), gc.reshape(b, t, hv), gdn_conv_w[j],
                                 gdn_o_norm[j], cstate, s0)
            new_s.append(s_fin)
            new_gbuf.append(proj3[:, t - (kc - 1):, :conv_dim])
            h = matmul_bias_res(o.reshape(n, hv * GDN_HEAD), pw["gdn_w_out"][j],
                                jnp.zeros((d,), F32), h)
        else:
            kw = conf_dw_w.shape[1]
            ch = conf_dw_w.shape[2]
            glu = rms_matmul_glu(h, norm_mix[i], pw["conf_w_in"][j], conf_b_in[j])
            glu3 = glu.reshape(b, t, ch)
            hpad = 32
            if has_state:
                prev = conf_buf[j]
                halo = jnp.pad(prev, ((0, 0), (hpad - (kw - 1), 0), (0, 0)))
                c = conf_conv(glu3, halo, conf_dw_w[j], conf_dw_b[j], conf_ln_g[j], conf_ln_b[j], bb=8)
                new_cbuf.append(jnp.concatenate([prev, glu3], axis=1)[:, t:])
            else:
                tmc = _tile(t, 256)
                nt = t // tmc
                tiles = glu3.reshape(b, nt, tmc, ch)
                halo = jnp.concatenate([jnp.zeros((b, 1, hpad, ch), F32), tiles[:, :-1, tmc - hpad:]], axis=1)
                c = conf_conv(tiles.reshape(b * nt, tmc, ch), halo.reshape(b * nt, hpad, ch),
                              conf_dw_w[j], conf_dw_b[j], conf_ln_g[j], conf_ln_b[j], bb=1)
                new_cbuf.append(glu3[:, t - (kw - 1):])
            h = matmul_bias_res(c.reshape(n, ch), pw["conf_w_out"][j], conf_b_out[j], h)
        q, xn = rms_matmul_xn(h, norm_ffn[i], pw["peer_w_q"][i])
        r2, e2, lc, co = peer_route(q, pw["peer_keys1"][i], pw["peer_keys2"][i])
        y = peer_mix(xn, pw["peer_u"][i], pw["peer_vt"][i], r2, e2, lc, co)
        h = ple_mix(h, y, norm_ple[i], pw["ple_w_gate"][i], ple_b_gate[i],
                    p[i].reshape(n, -1), pw["ple_w_proj"][i])
    out = rmsnorm(h, final_norm).reshape(b, t, d)
    return out, jnp.stack(new_s), jnp.stack(new_gbuf), jnp.stack(new_cbuf)


def kernel(x_prompt, x_sample, p_prompt, p_sample, state_gdn_recurrent, state_gdn_conv, state_conf_conv, norm_mix, norm_ffn, norm_ple, final_norm, gdn_w_in, gdn_conv_w, gdn_a_log, gdn_dt_bias, gdn_o_norm, gdn_w_out, conf_w_in, conf_b_in, conf_dw_w, conf_dw_b, conf_ln_g, conf_ln_b, conf_w_out, conf_b_out, peer_w_q, peer_keys1, peer_keys2, peer_u, peer_v, ple_w_proj, ple_w_gate, ple_b_gate):
    wts = (norm_mix, norm_ffn, norm_ple, final_norm,
           gdn_w_in, gdn_conv_w, gdn_a_log, gdn_dt_bias, gdn_o_norm, gdn_w_out,
           conf_w_in, conf_b_in, conf_dw_w, conf_dw_b, conf_ln_g, conf_ln_b, conf_w_out, conf_b_out,
           peer_w_q, peer_keys1, peer_keys2, peer_u, peer_v,
           ple_w_proj, ple_w_gate, ple_b_gate)
    pw = _prep_weights(*wts)
    y_p, s_p, gb_p, cb_p = _trunk(x_prompt, p_prompt, None, None, None, wts, pw)
    y_s, s_s, gb_s, cb_s = _trunk(x_sample, p_sample, state_gdn_recurrent, state_gdn_conv,
                                  state_conf_conv, wts, pw)
    return (y_p, y_s, s_p, s_s, gb_p, gb_s, cb_p, cb_s)
```

```python
import functools
import math

import jax
import jax.numpy as jnp
from jax import lax
from jax.experimental import pallas as pl
from jax.experimental.pallas import tpu as pltpu

F32 = jnp.float32
BF16 = jnp.bfloat16
EPS = 1e-6

GDN_HEAD = 128
GDN_CHUNK = 64
PEER_HEADS = 8
PEER_NKEYS = 128
PEER_HALF = 128
PEER_TOPK = 16

LANES = 128
SUBLANES = 8
VMEM_LIMIT = 56 * 1024 * 1024

_CANDS = [(a, b) for a in range(PEER_TOPK) for b in range(PEER_TOPK) if (a + 1) * (b + 1) <= PEER_TOPK]
_NCAND_PAD = -(-len(_CANDS) // SUBLANES) * SUBLANES


def _cparams(*sem):
    return pltpu.CompilerParams(dimension_semantics=sem, vmem_limit_bytes=VMEM_LIMIT)


def _tile(n, target):
    t = min(n, target)
    assert n % t == 0, (n, t)
    return t


def _dot(a, b):
    return jnp.dot(a.astype(BF16), b.astype(BF16), preferred_element_type=F32)


def _dot_nt(a, b):
    return lax.dot_general(a.astype(BF16), b.astype(BF16), (((1,), (1,)), ((), ())),
                           preferred_element_type=F32)


def _rms(x, g):
    return x * lax.rsqrt(jnp.mean(x * x, axis=-1, keepdims=True) + EPS) * g


def _sigmoid(x):
    return 1.0 / (1.0 + jnp.exp(-x))


def _silu(x):
    return x * _sigmoid(x)


def _rms_mm_kernel(x_ref, g_ref, w_ref, o_ref, xn_ref):
    @pl.when(pl.program_id(1) == 0)
    def _():
        xn_ref[...] = _rms(x_ref[...], g_ref[...]).astype(BF16)

    o_ref[...] = jnp.dot(xn_ref[...], w_ref[...], preferred_element_type=F32)


def rms_matmul(x, g, w, n_cols, tn=512, tm=1024):
    n, d = x.shape
    tm = _tile(n, tm)
    return pl.pallas_call(
        _rms_mm_kernel,
        grid=(n // tm, n_cols // tn),
        in_specs=[pl.BlockSpec((tm, d), lambda i, j: (i, 0)),
                  pl.BlockSpec((1, d), lambda i, j: (0, 0)),
                  pl.BlockSpec((d, tn), lambda i, j: (0, j))],
        out_specs=pl.BlockSpec((tm, tn), lambda i, j: (i, j)),
        out_shape=jax.ShapeDtypeStruct((n, n_cols), F32),
        scratch_shapes=[pltpu.VMEM((tm, d), BF16)],
        compiler_params=_cparams("parallel", "arbitrary"),
        name="rms_matmul",
    )(x, g.reshape(1, d), w)


def rms_matmul_xn(x, g, w, tn=512, tm=1024):
    n, d = x.shape
    n_cols = w.shape[1]
    tm = _tile(n, tm)
    return pl.pallas_call(
        _rms_mm_kernel,
        grid=(n // tm, n_cols // tn),
        in_specs=[pl.BlockSpec((tm, d), lambda i, j: (i, 0)),
                  pl.BlockSpec((1, d), lambda i, j: (0, 0)),
                  pl.BlockSpec((d, tn), lambda i, j: (0, j))],
        out_specs=[pl.BlockSpec((tm, tn), lambda i, j: (i, j)),
                   pl.BlockSpec((tm, d), lambda i, j: (i, 0))],
        out_shape=[jax.ShapeDtypeStruct((n, n_cols), F32),
                   jax.ShapeDtypeStruct((n, d), BF16)],
        compiler_params=_cparams("parallel", "arbitrary"),
        name="rms_matmul_xn",
    )(x, g.reshape(1, d), w)


def _mm_res_kernel(x_ref, w_ref, b_ref, r_ref, o_ref):
    o_ref[...] = (jnp.dot(x_ref[...], w_ref[...], preferred_element_type=F32)
                  + b_ref[...] + r_ref[...])


def matmul_bias_res(x, w, b, res, tn=512, tm=1024):
    n, k = x.shape
    n_cols = w.shape[1]
    tm = _tile(n, tm)
    return pl.pallas_call(
        _mm_res_kernel,
        grid=(n // tm, n_cols // tn),
        in_specs=[pl.BlockSpec((tm, k), lambda i, j: (i, 0)),
                  pl.BlockSpec((k, tn), lambda i, j: (0, j)),
                  pl.BlockSpec((1, tn), lambda i, j: (0, j)),
                  pl.BlockSpec((tm, tn), lambda i, j: (i, j))],
        out_specs=pl.BlockSpec((tm, tn), lambda i, j: (i, j)),
        out_shape=jax.ShapeDtypeStruct((n, n_cols), F32),
        compiler_params=_cparams("parallel", "arbitrary"),
        name="matmul_bias_res",
    )(x, w, b.reshape(1, n_cols), res)


def _glu_kernel(x_ref, g_ref, wa_ref, wb_ref, ba_ref, bb_ref, o_ref, xn_ref):
    @pl.when(pl.program_id(1) == 0)
    def _():
        xn_ref[...] = _rms(x_ref[...], g_ref[...]).astype(BF16)

    a = jnp.dot(xn_ref[...], wa_ref[...], preferred_element_type=F32) + ba_ref[...]
    b = jnp.dot(xn_ref[...], wb_ref[...], preferred_element_type=F32) + bb_ref[...]
    o_ref[...] = a * _sigmoid(b)


def rms_matmul_glu(x, g, w, b, tn=512, tm=1024):
    n, d = x.shape
    c = w.shape[1] // 2
    tm = _tile(n, tm)
    nj = c // tn
    b2 = b.reshape(1, 2 * c)
    return pl.pallas_call(
        _glu_kernel,
        grid=(n // tm, nj),
        in_specs=[pl.BlockSpec((tm, d), lambda i, j: (i, 0)),
                  pl.BlockSpec((1, d), lambda i, j: (0, 0)),
                  pl.BlockSpec((d, tn), lambda i, j: (0, j)),
                  pl.BlockSpec((d, tn), lambda i, j: (0, j + nj)),
                  pl.BlockSpec((1, tn), lambda i, j: (0, j)),
                  pl.BlockSpec((1, tn), lambda i, j: (0, j + nj))],
        out_specs=pl.BlockSpec((tm, tn), lambda i, j: (i, j)),
        out_shape=jax.ShapeDtypeStruct((n, c), F32),
        scratch_shapes=[pltpu.VMEM((tm, d), BF16)],
        compiler_params=_cparams("parallel", "arbitrary"),
        name="rms_matmul_glu",
    )(x, g.reshape(1, d), w, w, b2, b2)


def _ple_kernel(h_ref, y_ref, g_ref, wg_ref, bg_ref, p_ref, wp_ref, o_ref, xn_ref, hh_ref):
    j = pl.program_id(1)
    tn = o_ref.shape[1]

    @pl.when(j == 0)
    def _():
        hh = h_ref[...] + y_ref[...]
        xn_ref[...] = _rms(hh, g_ref[...]).astype(BF16)
        for jj in range(hh_ref.shape[0]):
            hh_ref[jj] = hh[:, jj * tn:(jj + 1) * tn]

    gate = _sigmoid(jnp.dot(xn_ref[...], wg_ref[...], preferred_element_type=F32) + bg_ref[...])
    proj = jnp.dot(p_ref[...].astype(BF16), wp_ref[...], preferred_element_type=F32)
    o_ref[...] = hh_ref[j] + proj * gate


def ple_mix(h, y, g, w_gate, b_gate, p, w_proj, tn=512, tm=512):
    n, d = h.shape
    pd = p.shape[1]
    tm = _tile(n, tm)
    return pl.pallas_call(
        _ple_kernel,
        grid=(n // tm, d // tn),
        in_specs=[pl.BlockSpec((tm, d), lambda i, j: (i, 0)),
                  pl.BlockSpec((tm, d), lambda i, j: (i, 0)),
                  pl.BlockSpec((1, d), lambda i, j: (0, 0)),
                  pl.BlockSpec((d, tn), lambda i, j: (0, j)),
                  pl.BlockSpec((1, tn), lambda i, j: (0, j)),
                  pl.BlockSpec((tm, pd), lambda i, j: (i, 0)),
                  pl.BlockSpec((pd, tn), lambda i, j: (0, j))],
        out_specs=pl.BlockSpec((tm, tn), lambda i, j: (i, j)),
        out_shape=jax.ShapeDtypeStruct((n, d), F32),
        scratch_shapes=[pltpu.VMEM((tm, d), BF16), pltpu.VMEM((d // tn, tm, tn), F32)],
        compiler_params=_cparams("parallel", "arbitrary"),
        name="ple_mix",
    )(h, y, g.reshape(1, d), w_gate, b_gate.reshape(1, d), p, w_proj)


def _rmsnorm_kernel(x_ref, g_ref, o_ref):
    o_ref[...] = _rms(x_ref[...], g_ref[...])


def rmsnorm(x, g, tm=512):
    n, d = x.shape
    tm = _tile(n, tm)
    return pl.pallas_call(
        _rmsnorm_kernel,
        grid=(n // tm,),
        in_specs=[pl.BlockSpec((tm, d), lambda i: (i, 0)),
                  pl.BlockSpec((1, d), lambda i: (0, 0))],
        out_specs=pl.BlockSpec((tm, d), lambda i: (i, 0)),
        out_shape=jax.ShapeDtypeStruct((n, d), F32),
        compiler_params=_cparams("parallel"),
        name="final_rmsnorm",
    )(x, g.reshape(1, d))


def _gdn_gates_kernel(x_ref, g_ref, wb_ref, wa_ref, alog_ref, dtb_ref, beta_ref, gc_ref, *, chunk):
    xn = _rms(x_ref[...], g_ref[...]).astype(BF16)
    b_raw = jnp.dot(xn, wb_ref[...], preferred_element_type=F32)
    a_raw = jnp.dot(xn, wa_ref[...], preferred_element_type=F32)
    beta_ref[...] = _sigmoid(b_raw)
    z = a_raw + dtb_ref[...]
    softplus = jnp.maximum(z, 0.0) + jnp.log(1.0 + jnp.exp(-jnp.abs(z)))
    gc = -jnp.exp(alog_ref[...]) * softplus
    pos = lax.broadcasted_iota(jnp.int32, gc.shape, 0) % chunk
    s = 1
    while s < chunk:
        gc = gc + jnp.where(pos >= s, pltpu.roll(gc, s, 0), 0.0)
        s *= 2
    gc_ref[...] = gc


def gdn_gates(x, g, w_b, w_a, a_log, dt_bias, chunk, tm=512):
    n, d = x.shape
    hv = w_b.shape[1]
    tm = _tile(n, tm)
    assert tm % chunk == 0
    return pl.pallas_call(
        functools.partial(_gdn_gates_kernel, chunk=chunk),
        grid=(n // tm,),
        in_specs=[pl.BlockSpec((tm, d), lambda i: (i, 0)),
                  pl.BlockSpec((1, d), lambda i: (0, 0)),
                  pl.BlockSpec((d, hv), lambda i: (0, 0)),
                  pl.BlockSpec((d, hv), lambda i: (0, 0)),
                  pl.BlockSpec((1, hv), lambda i: (0, 0)),
                  pl.BlockSpec((1, hv), lambda i: (0, 0))],
        out_specs=[pl.BlockSpec((tm, hv), lambda i: (i, 0)),
                   pl.BlockSpec((tm, hv), lambda i: (i, 0))],
        out_shape=[jax.ShapeDtypeStruct((n, hv), F32),
                   jax.ShapeDtypeStruct((n, hv), F32)],
        compiler_params=_cparams("parallel"),
        name="gdn_gates",
    )(x, g.reshape(1, d), w_b, w_a, a_log.reshape(1, hv), dt_bias.reshape(1, hv))


def _conv_silu(x, w_ref, prev):
    kconv = w_ref.shape[0]
    tpos = lax.broadcasted_iota(jnp.int32, x.shape, 0)
    y = x * w_ref[kconv - 1:kconv, :]
    for s in range(1, kconv):
        tail = 0.0 if prev is None else pltpu.roll(prev, s, 0)
        shifted = jnp.where(tpos >= s, pltpu.roll(x, s, 0), tail)
        y = y + shifted * w_ref[kconv - 1 - s:kconv - s, :]
    return _silu(y)


def _l2n(x):
    return x * lax.rsqrt(jnp.sum(x * x, axis=-1, keepdims=True) + EPS)


def _head_column(x, head):
    lane = lax.broadcasted_iota(jnp.int32, x.shape, 1)
    col = jnp.sum(jnp.where(lane == head, x, 0.0), axis=1, keepdims=True)
    return jnp.broadcast_to(col, (x.shape[0], GDN_HEAD))


def _gdn_chunk_pre(chains, c, between=None):
    def tick():
        if between is not None:
            between()

    row = lax.broadcasted_iota(jnp.int32, (c, c), 0)
    col = lax.broadcasted_iota(jnp.int32, (c, c), 1)
    eye = jnp.where(row == col, 1.0, 0.0)
    st = []
    for (q, k, v, bet, gcb) in chains:
        gi = gcb[:, :c]
        gj = gcb.T[:c, :]
        decay = jnp.exp(jnp.where(row >= col, gi - gj, -jnp.inf))
        egc = jnp.exp(gcb)
        kb = k * bet
        st.append(dict(q=q, k=k, gcb=gcb, decay=decay, egc=egc, kb=kb, vb=v * bet))
    for s in st:
        s["kq"] = _dot_nt(jnp.concatenate([s["kb"], s["q"]], axis=0), s["k"])
    tick()
    for s in st:
        s["attn"] = s["kq"][c:] * s["decay"]
        s["p"] = -jnp.where(row > col, s["kq"][:c] * s["decay"], 0.0)
        s["x"] = eye + s["p"]
    for s in st:
        s["p"] = _dot(s["p"], s["p"])
    tick()
    for _ in range(int(math.log2(c)) - 2):
        for s in st:
            s["m"] = _dot(jnp.concatenate([s["p"], s["x"]], axis=0), s["p"])
        tick()
        for s in st:
            s["p"] = s["m"][:c]
            s["x"] = s["x"] + s["m"][c:]
    for s in st:
        s["m"] = _dot(s["x"], s["p"])
    tick()
    for s in st:
        s["x"] = s["x"] + s["m"]
        s["rhs"] = jnp.concatenate([s["vb"], s["kb"] * s["egc"]], axis=1)
    for s in st:
        s["uw"] = _dot(s["x"], s["rhs"])
    tick()
    for s in st:
        s["glast"] = s["gcb"][c - 1:c, :]
        kdect = (s["k"] * jnp.exp(s["glast"] - s["gcb"])).T
        s["t"] = _dot(jnp.concatenate([kdect, s["attn"]], axis=0), s["uw"])
    tick()
    out = []
    hd = GDN_HEAD
    for s in st:
        t = s["t"]
        m = jnp.concatenate([-t[:hd, hd:], s["q"] * s["egc"] - t[hd:, hd:]], axis=0)
        out.append((m, t[:hd, :hd], t[hd:, :hd], jnp.exp(s["glast"])))
    return out


def _gdn_chunk_step(chains, c):
    rs = [_dot(m, s_prev) for (m, b, d, eg, s_prev) in chains]
    return [(r[GDN_HEAD:] + d, s_prev * eg + r[:GDN_HEAD] + b)
            for (m, b, d, eg, s_prev), r in zip(chains, rs)]


def _gdn_seq_kernel(q_ref, k_ref, v_ref, z_ref, beta_ref, gc_ref, cw_q_ref, cw_k_ref, cw_v_ref,
                    onorm_ref, o_ref, s_ref,
                    qs_ref, ks_ref, vs_ref, bs_ref, gs_ref, m_ref, b_ref, d_ref, eg_ref, st_ref,
                    *, t, c, group):
    hk = pl.program_id(1)
    n_chunks = t // c
    hd = GDN_HEAD
    qs_ref[...] = _l2n(_conv_silu(q_ref[0], cw_q_ref, None)) * (hd ** -0.5)
    ks_ref[...] = _l2n(_conv_silu(k_ref[0], cw_k_ref, None))
    vs_ref[...] = _conv_silu(v_ref[0], cw_v_ref, None)
    for i in range(2):
        bs_ref[i] = _head_column(beta_ref[0], 2 * hk + i)
        gs_ref[i] = _head_column(gc_ref[0], 2 * hk + i)
        st_ref[i] = jnp.zeros((hd, hd), F32)

    def rows(n, size):
        start = n * size
        return pl.ds(start if isinstance(start, int) else pl.multiple_of(start, size), size)

    def pre_group(it, between):
        loaded = []
        for g in range(group):
            n = it * group + g
            q = qs_ref[rows(n, c), :]
            k = ks_ref[rows(n, c), :]
            for i in range(2):
                loaded.append((n, i, q, k, vs_ref[rows(n, c), i * hd:(i + 1) * hd],
                               bs_ref[i, rows(n, c), :], gs_ref[i, rows(n, c), :]))
        pre_out = _gdn_chunk_pre([x[2:] for x in loaded], c, between)
        for (n, i, *_), (m, b, d, eg) in zip(loaded, pre_out):
            m_ref[i, rows(n, c + hd), :] = m.astype(BF16)
            b_ref[i, rows(n, hd), :] = b
            d_ref[i, rows(n, c), :] = d
            eg_ref[i, rows(n, SUBLANES), :] = jnp.broadcast_to(eg, (SUBLANES, hd))

    def step_chunk(n):
        chains = []
        for i in range(2):
            chains.append((m_ref[i, rows(n, c + hd), :], b_ref[i, rows(n, hd), :],
                           d_ref[i, rows(n, c), :], eg_ref[i, rows(n, SUBLANES), :][0:1],
                           st_ref[i]))
        for i, (o, s_new) in enumerate(_gdn_chunk_step(chains, c)):
            st_ref[i] = s_new
            d_ref[i, rows(n, c), :] = o

    n_groups = n_chunks // group
    pre_group(0, None)

    def body(it, carry):
        pending = list(range(group))

        def between():
            if pending:
                step_chunk((it - 1) * group + pending.pop(0))

        pre_group(it, between)
        while pending:
            step_chunk((it - 1) * group + pending.pop(0))
        return carry

    if n_groups <= 4:
        for it in range(1, n_groups):
            body(it, 0)
    else:
        lax.fori_loop(1, n_groups, body, 0)
    for g in range(group):
        step_chunk((n_groups - 1) * group + g)

    for i in range(2):
        s_ref[0, i] = st_ref[i]
        gated = _rms(d_ref[i], onorm_ref[...]) * _silu(z_ref[0, :, i * hd:(i + 1) * hd])
        o_ref[0, :, i * hd:(i + 1) * hd] = gated.astype(BF16)


def _gdn_step_kernel(q_ref, k_ref, v_ref, z_ref, beta_ref, gc_ref, cw_q_ref, cw_k_ref, cw_v_ref,
                     onorm_ref, pq_ref, pk_ref, pv_ref, s0_ref, o_ref, s_ref, *, c, bb, group):
    hk = pl.program_id(1)
    hd = GDN_HEAD

    def per_group(it, carry):
        loaded = []
        for g in range(group):
            bi = it * group + g
            loaded.append((bi, q_ref[bi], k_ref[bi], v_ref[bi], pq_ref[bi], pk_ref[bi], pv_ref[bi],
                           beta_ref[bi], gc_ref[bi], z_ref[bi], s0_ref[bi, 0], s0_ref[bi, 1]))
        chains, meta = [], []
        for (bi, xq, xk, xv, pq, pk, pv, beta, gc, z, s0a, s0b) in loaded:
            q = _l2n(_conv_silu(xq, cw_q_ref, pq)) * (hd ** -0.5)
            k = _l2n(_conv_silu(xk, cw_k_ref, pk))
            vv = _conv_silu(xv, cw_v_ref, pv)
            for i, s0 in enumerate((s0a, s0b)):
                chains.append((q, k, vv[:, i * hd:(i + 1) * hd],
                               _head_column(beta, 2 * hk + i), _head_column(gc, 2 * hk + i)))
                meta.append((bi, i, z[:, i * hd:(i + 1) * hd], s0))
        pre_out = _gdn_chunk_pre(chains, c)
        step_out = _gdn_chunk_step([p + (m[3],) for p, m in zip(pre_out, meta)], c)
        for (bi, i, z, _), (o, s_new) in zip(meta, step_out):
            s_ref[bi, i] = s_new
            gated = _rms(o, onorm_ref[...]) * _silu(z)
            o_ref[bi, :, i * hd:(i + 1) * hd] = gated.astype(BF16)
        return carry

    lax.fori_loop(0, bb // group, per_group, 0)


def gdn_delta(proj, beta, gc, conv_w, o_norm, conv_state, s0):
    b, t, _ = proj.shape
    hv = beta.shape[2]
    hq = hv // 2
    c = min(GDN_CHUNK, t)
    assert t % c == 0
    n_chunks = t // c
    kc = conv_w.shape[0]
    hd = GDN_HEAD
    q0, k0, v0, z0 = 0, hq, hq, 2 * hq
    bb = 1 if s0 is None else _tile(b, 16)

    in_specs = [pl.BlockSpec((bb, t, hd), lambda i, h: (i, 0, q0 + h)),
                pl.BlockSpec((bb, t, hd), lambda i, h: (i, 0, k0 + h)),
                pl.BlockSpec((bb, t, 2 * hd), lambda i, h: (i, 0, v0 + h)),
                pl.BlockSpec((bb, t, 2 * hd), lambda i, h: (i, 0, z0 + h)),
                pl.BlockSpec((bb, t, hv), lambda i, h: (i, 0, 0)),
                pl.BlockSpec((bb, t, hv), lambda i, h: (i, 0, 0)),
                pl.BlockSpec((kc, hd), lambda i, h: (0, q0 + h)),
                pl.BlockSpec((kc, hd), lambda i, h: (0, k0 + h)),
                pl.BlockSpec((kc, 2 * hd), lambda i, h: (0, v0 + h)),
                pl.BlockSpec((1, hd), lambda i, h: (0, 0))]
    args = [proj, proj, proj, proj, beta, gc, conv_w, conv_w, conv_w, o_norm.reshape(1, hd)]
    out_specs = [pl.BlockSpec((bb, t, 2 * hd), lambda i, h: (i, 0, h)),
                 pl.BlockSpec((bb, 2, hd, hd), lambda i, h: (i, h, 0, 0))]
    out_shape = [jax.ShapeDtypeStruct((b, t, hv * hd), BF16),
                 jax.ShapeDtypeStruct((b, hv, hd, hd), F32)]
    if s0 is None:
        group = 8 if n_chunks % 8 == 0 else (4 if n_chunks % 4 == 0 else 1)
        body = functools.partial(_gdn_seq_kernel, t=t, c=c, group=group)
        scratch = [pltpu.VMEM((t, hd), F32), pltpu.VMEM((t, hd), F32), pltpu.VMEM((t, 2 * hd), F32),
                   pltpu.VMEM((2, t, hd), F32), pltpu.VMEM((2, t, hd), F32),
                   pltpu.VMEM((2, n_chunks * (c + hd), hd), BF16),
                   pltpu.VMEM((2, n_chunks * hd, hd), F32),
                   pltpu.VMEM((2, t, hd), F32),
                   pltpu.VMEM((2, n_chunks * SUBLANES, hd), F32),
                   pltpu.VMEM((2, hd, hd), F32)]
    else:
        assert n_chunks == 1
        group = 16 if bb % 16 == 0 else (8 if bb % 8 == 0 else 1)
        body = functools.partial(_gdn_step_kernel, c=c, bb=bb, group=group)
        scratch = []
        in_specs += [pl.BlockSpec((bb, SUBLANES, hd), lambda i, h: (i, 0, q0 + h)),
                     pl.BlockSpec((bb, SUBLANES, hd), lambda i, h: (i, 0, k0 + h)),
                     pl.BlockSpec((bb, SUBLANES, 2 * hd), lambda i, h: (i, 0, v0 + h)),
                     pl.BlockSpec((bb, 2, hd, hd), lambda i, h: (i, h, 0, 0))]
        args += [conv_state, conv_state, conv_state, s0]
    return pl.pallas_call(
        body,
        grid=(b // bb, hq),
        in_specs=in_specs,
        out_specs=out_specs,
        out_shape=out_shape,
        scratch_shapes=scratch,
        compiler_params=_cparams("parallel", "parallel"),
        name="gdn_delta",
    )(*args)


def _conf_conv_kernel(x_ref, halo_ref, w_ref, b_ref, lg_ref, lb_ref, o_ref, xx_ref, *, bb, tm, kw, hpad):
    base = hpad - (kw - 1)
    span = tm + SUBLANES

    def per_batch(bi, carry):
        xx_ref[0:hpad, :] = halo_ref[bi]
        xx_ref[hpad:hpad + tm, :] = x_ref[bi]
        xx_ref[hpad + tm:hpad + span, :] = jnp.zeros((SUBLANES, xx_ref.shape[1]), F32)
        ch = xx_ref.shape[1]
        cblk = min(ch, 4 * LANES)
        parts = []
        for c0 in range(0, ch, cblk):
            acc = None
            for r in range(SUBLANES):
                z = None
                for j in range(kw):
                    if (base + j) % SUBLANES != r:
                        continue
                    a = base + j - r
                    term = xx_ref[a:a + span, c0:c0 + cblk] * w_ref[j:j + 1, c0:c0 + cblk]
                    z = term if z is None else z + term
                if z is None:
                    continue
                if r:
                    z = pltpu.roll(z, span - r, 0)
                acc = z[:tm] if acc is None else acc + z[:tm]
            parts.append(acc)
        acc = jnp.concatenate(parts, axis=1) + b_ref[...]
        mu = jnp.mean(acc, axis=-1, keepdims=True)
        xc = acc - mu
        y = xc * lax.rsqrt(jnp.mean(xc * xc, axis=-1, keepdims=True) + EPS)
        y = y * lg_ref[...] + lb_ref[...]
        o_ref[bi] = _silu(y).astype(BF16)
        return carry

    lax.fori_loop(0, bb, per_batch, 0)


def conf_conv(x, halo, w, b, ln_g, ln_b, bb):
    nb, tm, ch = x.shape
    hpad = halo.shape[1]
    kw = w.shape[0]
    assert nb % bb == 0 and hpad >= kw - 1
    return pl.pallas_call(
        functools.partial(_conf_conv_kernel, bb=bb, tm=tm, kw=kw, hpad=hpad),
        grid=(nb // bb,),
        in_specs=[pl.BlockSpec((bb, tm, ch), lambda i: (i, 0, 0)),
                  pl.BlockSpec((bb, hpad, ch), lambda i: (i, 0, 0)),
                  pl.BlockSpec((kw, ch), lambda i: (0, 0)),
                  pl.BlockSpec((1, ch), lambda i: (0, 0)),
                  pl.BlockSpec((1, ch), lambda i: (0, 0)),
                  pl.BlockSpec((1, ch), lambda i: (0, 0))],
        out_specs=pl.BlockSpec((bb, tm, ch), lambda i: (i, 0, 0)),
        out_shape=jax.ShapeDtypeStruct((nb, tm, ch), BF16),
        scratch_shapes=[pltpu.VMEM((hpad + tm + SUBLANES, ch), F32)],
        compiler_params=_cparams("parallel"),
        name="conf_conv",
    )(x, halo, w, b.reshape(1, ch), ln_g.reshape(1, ch), ln_b.reshape(1, ch))


def _extract_top(s, rounds):
    rows = s.shape[0]
    iota = lax.broadcasted_iota(jnp.int32, s.shape, 0)
    rank = jnp.full(s.shape, float(rounds), F32)
    vals = []
    for r in range(rounds):
        m = jnp.max(s, axis=0, keepdims=True)
        idx = jnp.min(jnp.where(s == m, iota, rows), axis=0, keepdims=True)
        sel = iota == idx
        rank = jnp.where(sel, float(r), rank)
        s = jnp.where(sel, -jnp.inf, s)
        vals.append(m)
    return rank, vals


def _peer_route_kernel(q_ref, k1_ref, k2_ref, r2_ref, e2_ref, lc_ref, co_ref, cand_ref):
    s1 = _dot_nt(k1_ref[...], q_ref[:, :PEER_HALF])
    s2 = _dot_nt(k2_ref[...], q_ref[:, PEER_HALF:])
    rank1, v1 = _extract_top(s1, PEER_TOPK)
    rank2, v2 = _extract_top(s2, PEER_TOPK)
    cand_ref[...] = jnp.full(cand_ref.shape, -jnp.inf, F32)
    for r, (a, b) in enumerate(_CANDS):
        cand_ref[r:r + 1, :] = v1[a] + v2[b]
    cand = cand_ref[...]
    crank, cvals = _extract_top(cand, PEER_TOPK)
    sel = jnp.where(crank < float(PEER_TOPK), 1.0, 0.0)
    zsum = jnp.sum(sel * jnp.exp(cand - cvals[0]), axis=0, keepdims=True)
    crow = lax.broadcasted_iota(jnp.int32, cand.shape, 0)
    lc = jnp.zeros_like(s1)
    pos = 0
    for a in range(PEER_TOPK):
        cnt = sum(1 for (aa, _) in _CANDS if aa == a)
        in_a = (crow >= pos) & (crow < pos + cnt)
        la = jnp.sum(jnp.where(in_a, sel, 0.0), axis=0, keepdims=True)
        lc = lc + jnp.where(rank1 == float(a), la, 0.0)
        pos += cnt
    r2_ref[0] = rank2
    e2_ref[0] = jnp.exp(s2 - v2[0])
    lc_ref[0] = lc
    co_ref[0] = jnp.exp(s1 - v1[0]) / zsum


def peer_route(q, keys1, keys2, tm=512):
    n, qd = q.shape
    tm = _tile(n, tm)
    nk = keys1.shape[0]
    assert qd == PEER_HEADS * 2 * PEER_HALF
    tab = jax.ShapeDtypeStruct((PEER_HEADS, nk, n), F32)
    tspec = pl.BlockSpec((1, nk, tm), lambda i, h: (h, 0, i))
    return pl.pallas_call(
        _peer_route_kernel,
        grid=(n // tm, PEER_HEADS),
        in_specs=[pl.BlockSpec((tm, 2 * PEER_HALF), lambda i, h: (i, h)),
                  pl.BlockSpec(keys1.shape, lambda i, h: (0, 0)),
                  pl.BlockSpec(keys2.shape, lambda i, h: (0, 0))],
        out_specs=[tspec, tspec, tspec, tspec],
        out_shape=[tab, tab, tab, tab],
        scratch_shapes=[pltpu.VMEM((_NCAND_PAD, tm), F32)],
        compiler_params=_cparams("parallel", "parallel"),
        name="peer_route",
    )(q, keys1, keys2)


def _peer_mix_kernel(xn_ref, u_ref, vt_ref, r2_ref, e2_ref, lc_ref, co_ref, o_ref,
                     acc_ref, a_ref, *, eb, sub):
    j = pl.program_id(1)
    nk = PEER_NKEYS
    nsub = eb // sub
    tm = xn_ref.shape[0]
    ct = LANES

    @pl.when(j == 0)
    def _():
        acc_ref[...] = jnp.zeros_like(acc_ref)

    def hidden(s):
        return _dot_nt(u_ref[s * sub:(s + 1) * sub, :], xn_ref[...])

    def gated(s, ht):
        for ii in range(sub // nk):
            i1 = j * (eb // nk) + s * (sub // nk) + ii
            lrows = [lc_ref[h, pl.ds(i1, 1), :] for h in range(PEER_HEADS)]
            crows = [co_ref[h, pl.ds(i1, 1), :] for h in range(PEER_HEADS)]
            for c0 in range(0, tm, ct):
                hv = ht[ii * nk:(ii + 1) * nk, c0:c0 + ct]
                act = 0.5 * hv * (1.0 + lax.erf(hv * (2.0 ** -0.5)))
                gate = None
                for h in range(PEER_HEADS):
                    term = jnp.where(r2_ref[h, :, c0:c0 + ct] < lrows[h][:, c0:c0 + ct],
                                     e2_ref[h, :, c0:c0 + ct] * crows[h][:, c0:c0 + ct], 0.0)
                    gate = term if gate is None else gate + term
                a_ref[s, ii * nk:(ii + 1) * nk, c0:c0 + ct] = (gate * act).astype(BF16)

    hts = [hidden(s) for s in range(nsub)]
    for s in range(nsub):
        gated(s, hts[s])
        acc_ref[...] += jnp.dot(vt_ref[:, s * sub:(s + 1) * sub], a_ref[s], preferred_element_type=F32)

    @pl.when(j == pl.num_programs(1) - 1)
    def _():
        o_ref[...] = acc_ref[...].T


def peer_mix(xn, u, vt, layer, r2, e2, lc, co, tm=512, eb=1024, sub=256):
    n, d = xn.shape
    ne = u.shape[1]
    tm = _tile(n, tm)
    tspec = pl.BlockSpec((PEER_HEADS, PEER_NKEYS, tm), lambda i, j: (0, 0, i))
    return pl.pallas_call(
        functools.partial(_peer_mix_kernel, eb=eb, sub=sub),
        grid=(n // tm, ne // eb),
        in_specs=[pl.BlockSpec((tm, d), lambda i, j: (i, 0)),
                  pl.BlockSpec((None, eb, d), lambda i, j: (layer, j, 0)),
                  pl.BlockSpec((None, d, eb), lambda i, j: (layer, 0, j)),
                  tspec, tspec, tspec, tspec],
        out_specs=pl.BlockSpec((tm, d), lambda i, j: (i, 0)),
        out_shape=jax.ShapeDtypeStruct((n, d), F32),
        scratch_shapes=[pltpu.VMEM((d, tm), F32), pltpu.VMEM((eb // sub, sub, tm), BF16)],
        compiler_params=_cparams("parallel", "arbitrary"),
        name="peer_mix",
    )(xn, u, vt, r2, e2, lc, co)


def _cast_transpose_kernel(x_ref, o_ref):
    o_ref[0] = x_ref[0].T.astype(BF16)


def cast_transpose(x, tr=512):
    nl, r, c = x.shape
    tr = _tile(r, tr)
    return pl.pallas_call(
        _cast_transpose_kernel,
        grid=(nl, r // tr),
        in_specs=[pl.BlockSpec((1, tr, c), lambda l, i: (l, i, 0))],
        out_specs=pl.BlockSpec((1, c, tr), lambda l, i: (l, 0, i)),
        out_shape=jax.ShapeDtypeStruct((nl, c, r), BF16),
        compiler_params=_cparams("parallel", "parallel"),
        name="cast_transpose",
    )(x)


def _prep_weights(norm_mix, norm_ffn, norm_ple, final_norm,
                  gdn_w_in, gdn_conv_w, gdn_a_log, gdn_dt_bias, gdn_o_norm, gdn_w_out,
                  conf_w_in, conf_b_in, conf_dw_w, conf_dw_b, conf_ln_g, conf_ln_b, conf_w_out, conf_b_out,
                  peer_w_q, peer_keys1, peer_keys2, peer_u, peer_v,
                  ple_w_proj, ple_w_gate, ple_b_gate):
    hv = gdn_a_log.shape[1]
    main = gdn_w_in.shape[2] - 2 * hv
    return dict(
        gdn_w_main=gdn_w_in[:, :, :main].astype(BF16),
        gdn_w_b=gdn_w_in[:, :, main:main + hv].astype(BF16),
        gdn_w_a=gdn_w_in[:, :, main + hv:].astype(BF16),
        gdn_w_out=gdn_w_out.astype(BF16),
        conf_w_in=conf_w_in.astype(BF16),
        conf_w_out=conf_w_out.astype(BF16),
        peer_w_q=peer_w_q.astype(BF16),
        peer_keys1=peer_keys1.astype(BF16),
        peer_keys2=peer_keys2.astype(BF16),
        peer_u=peer_u.astype(BF16),
        peer_vt=cast_transpose(peer_v),
        ple_w_proj=ple_w_proj.astype(BF16),
        ple_w_gate=ple_w_gate.astype(BF16),
    )


def _trunk(x, p, gdn_s, gdn_buf, conf_buf, wts, pw):
    (norm_mix, norm_ffn, norm_ple, final_norm,
     gdn_w_in, gdn_conv_w, gdn_a_log, gdn_dt_bias, gdn_o_norm, gdn_w_out,
     conf_w_in, conf_b_in, conf_dw_w, conf_dw_b, conf_ln_g, conf_ln_b, conf_w_out, conf_b_out,
     peer_w_q, peer_keys1, peer_keys2, peer_u, peer_v,
     ple_w_proj, ple_w_gate, ple_b_gate) = wts
    b, t, d = x.shape
    n = b * t
    depth = norm_mix.shape[0]
    has_state = gdn_s is not None
    h = x.reshape(n, d)
    new_s, new_gbuf, new_cbuf = [], [], []
    for i in range(depth):
        j = i // 2
        if i % 2 == 0:
            hv = gdn_a_log.shape[1]
            conv_dim = gdn_conv_w.shape[2]
            kc = gdn_conv_w.shape[1]
            main = pw["gdn_w_main"].shape[2]
            proj = rms_matmul(h, norm_mix[i], pw["gdn_w_main"][j], main, tn=1024)
            chunk = min(GDN_CHUNK, t)
            beta, gc = gdn_gates(h, norm_mix[i], pw["gdn_w_b"][j], pw["gdn_w_a"][j],
                                 gdn_a_log[j], gdn_dt_bias[j], chunk)
            proj3 = proj.reshape(b, t, main)
            if has_state:
                cstate = jnp.pad(gdn_buf[j], ((0, 0), (SUBLANES - (kc - 1), 0), (0, 0)))
                s0 = gdn_s[j]
            else:
                cstate, s0 = None, None
            o, s_fin = gdn_delta(proj3, beta.reshape(b, t, hv), gc.reshape(b, t, hv), gdn_conv_w[j],
                                 gdn_o_norm[j], cstate, s0)
            new_s.append(s_fin)
            new_gbuf.append(proj3[:, t - (kc - 1):, :conv_dim])
            h = matmul_bias_res(o.reshape(n, hv * GDN_HEAD), pw["gdn_w_out"][j],
                                jnp.zeros((d,), F32), h)
        else:
            kw = conf_dw_w.shape[1]
            ch = conf_dw_w.shape[2]
            glu = rms_matmul_glu(h, norm_mix[i], pw["conf_w_in"][j], conf_b_in[j])
            glu3 = glu.reshape(b, t, ch)
            hpad = 32
            if has_state:
                prev = conf_buf[j]
                halo = jnp.pad(prev, ((0, 0), (hpad - (kw - 1), 0), (0, 0)))
                c = conf_conv(glu3, halo, conf_dw_w[j], conf_dw_b[j], conf_ln_g[j], conf_ln_b[j], bb=8)
                new_cbuf.append(jnp.concatenate([prev, glu3], axis=1)[:, t:])
            else:
                tmc = _tile(t, 256)
                nt = t // tmc
                tiles = glu3.reshape(b, nt, tmc, ch)
                halo = jnp.concatenate([jnp.zeros((b, 1, hpad, ch), F32), tiles[:, :-1, tmc - hpad:]], axis=1)
                c = conf_conv(tiles.reshape(b * nt, tmc, ch), halo.reshape(b * nt, hpad, ch),
                              conf_dw_w[j], conf_dw_b[j], conf_ln_g[j], conf_ln_b[j], bb=1)
                new_cbuf.append(glu3[:, t - (kw - 1):])
            h = matmul_bias_res(c.reshape(n, ch), pw["conf_w_out"][j], conf_b_out[j], h)
        q, xn = rms_matmul_xn(h, norm_ffn[i], pw["peer_w_q"][i])
        r2, e2, lc, co = peer_route(q, pw["peer_keys1"][i], pw["peer_keys2"][i])
        y = peer_mix(xn, pw["peer_u"], pw["peer_vt"], i, r2, e2, lc, co)
        h = ple_mix(h, y, norm_ple[i], pw["ple_w_gate"][i], ple_b_gate[i],
                    p[i].reshape(n, -1), pw["ple_w_proj"][i])
    out = rmsnorm(h, final_norm).reshape(b, t, d)
    return out, jnp.stack(new_s), jnp.stack(new_gbuf), jnp.stack(new_cbuf)


def kernel(x_prompt, x_sample, p_prompt, p_sample, state_gdn_recurrent, state_gdn_conv, state_conf_conv, norm_mix, norm_ffn, norm_ple, final_norm, gdn_w_in, gdn_conv_w, gdn_a_log, gdn_dt_bias, gdn_o_norm, gdn_w_out, conf_w_in, conf_b_in, conf_dw_w, conf_dw_b, conf_ln_g, conf_ln_b, conf_w_out, conf_b_out, peer_w_q, peer_keys1, peer_keys2, peer_u, peer_v, ple_w_proj, ple_w_gate, ple_b_gate):
    wts = (norm_mix, norm_ffn, norm_ple, final_norm,
           gdn_w_in, gdn_conv_w, gdn_a_log, gdn_dt_bias, gdn_o_norm, gdn_w_out,
           conf_w_in, conf_b_in, conf_dw_w, conf_dw_b, conf_ln_g, conf_ln_b, conf_w_out, conf_b_out,
           peer_w_q, peer_keys1, peer_keys2, peer_u, peer_v,
           ple_w_proj, ple_w_gate, ple_b_gate)
    pw = _prep_weights(*wts)
    y_p, s_p, gb_p, cb_p = _trunk(x_prompt, p_prompt, None, None, None, wts, pw)
    y_s, s_s, gb_s, cb_s = _trunk(x_sample, p_sample, state_gdn_recurrent, state_gdn_conv,
                                  state_conf_conv, wts, pw)
    return (y_p, y_s, s_p, s_s, gb_p, gb_s, cb_p, cb_s)
```

```python
import functools
import math

import jax
import jax.numpy as jnp
from jax import lax
from jax.experimental import pallas as pl
from jax.experimental.pallas import tpu as pltpu

F32 = jnp.float32
BF16 = jnp.bfloat16
EPS = 1e-6

GDN_HEAD = 128
GDN_CHUNK = 64
PEER_HEADS = 8
PEER_NKEYS = 128
PEER_HALF = 128
PEER_TOPK = 16

LANES = 128
SUBLANES = 8
VMEM_LIMIT = 56 * 1024 * 1024

_CANDS = [(a, b) for a in range(PEER_TOPK) for b in range(PEER_TOPK) if (a + 1) * (b + 1) <= PEER_TOPK]
_NCAND_PAD = -(-len(_CANDS) // SUBLANES) * SUBLANES


def _cparams(*sem):
    return pltpu.CompilerParams(dimension_semantics=sem, vmem_limit_bytes=VMEM_LIMIT)


def _tile(n, target):
    t = min(n, target)
    assert n % t == 0, (n, t)
    return t


def _dot(a, b):
    return jnp.dot(a.astype(BF16), b.astype(BF16), preferred_element_type=F32)


def _dot_nt(a, b):
    return lax.dot_general(a.astype(BF16), b.astype(BF16), (((1,), (1,)), ((), ())),
                           preferred_element_type=F32)


def _rms(x, g):
    return x * lax.rsqrt(jnp.mean(x * x, axis=-1, keepdims=True) + EPS) * g


def _sigmoid(x):
    return 1.0 / (1.0 + jnp.exp(-x))


def _silu(x):
    return x * _sigmoid(x)


def _rms_mm_kernel(x_ref, g_ref, w_ref, o_ref, xn_ref):
    @pl.when(pl.program_id(1) == 0)
    def _():
        xn_ref[...] = _rms(x_ref[...], g_ref[...]).astype(BF16)

    o_ref[...] = jnp.dot(xn_ref[...], w_ref[...], preferred_element_type=F32)


def rms_matmul(x, g, w, n_cols, tn=512, tm=1024):
    n, d = x.shape
    tm = _tile(n, tm)
    return pl.pallas_call(
        _rms_mm_kernel,
        grid=(n // tm, n_cols // tn),
        in_specs=[pl.BlockSpec((tm, d), lambda i, j: (i, 0)),
                  pl.BlockSpec((1, d), lambda i, j: (0, 0)),
                  pl.BlockSpec((d, tn), lambda i, j: (0, j))],
        out_specs=pl.BlockSpec((tm, tn), lambda i, j: (i, j)),
        out_shape=jax.ShapeDtypeStruct((n, n_cols), F32),
        scratch_shapes=[pltpu.VMEM((tm, d), BF16)],
        compiler_params=_cparams("parallel", "arbitrary"),
        name="rms_matmul",
    )(x, g.reshape(1, d), w)


def rms_matmul_xn(x, g, w, tn=512, tm=1024):
    n, d = x.shape
    n_cols = w.shape[1]
    tm = _tile(n, tm)
    return pl.pallas_call(
        _rms_mm_kernel,
        grid=(n // tm, n_cols // tn),
        in_specs=[pl.BlockSpec((tm, d), lambda i, j: (i, 0)),
                  pl.BlockSpec((1, d), lambda i, j: (0, 0)),
                  pl.BlockSpec((d, tn), lambda i, j: (0, j))],
        out_specs=[pl.BlockSpec((tm, tn), lambda i, j: (i, j)),
                   pl.BlockSpec((tm, d), lambda i, j: (i, 0))],
        out_shape=[jax.ShapeDtypeStruct((n, n_cols), F32),
                   jax.ShapeDtypeStruct((n, d), BF16)],
        compiler_params=_cparams("parallel", "arbitrary"),
        name="rms_matmul_xn",
    )(x, g.reshape(1, d), w)


def _mm_res_kernel(x_ref, w_ref, b_ref, r_ref, o_ref):
    o_ref[...] = (jnp.dot(x_ref[...], w_ref[...], preferred_element_type=F32)
                  + b_ref[...] + r_ref[...])


def matmul_bias_res(x, w, b, res, tn=512, tm=1024):
    n, k = x.shape
    n_cols = w.shape[1]
    tm = _tile(n, tm)
    return pl.pallas_call(
        _mm_res_kernel,
        grid=(n // tm, n_cols // tn),
        in_specs=[pl.BlockSpec((tm, k), lambda i, j: (i, 0)),
                  pl.BlockSpec((k, tn), lambda i, j: (0, j)),
                  pl.BlockSpec((1, tn), lambda i, j: (0, j)),
                  pl.BlockSpec((tm, tn), lambda i, j: (i, j))],
        out_specs=pl.BlockSpec((tm, tn), lambda i, j: (i, j)),
        out_shape=jax.ShapeDtypeStruct((n, n_cols), F32),
        compiler_params=_cparams("parallel", "arbitrary"),
        name="matmul_bias_res",
    )(x, w, b.reshape(1, n_cols), res)


def _glu_kernel(x_ref, g_ref, wa_ref, wb_ref, ba_ref, bb_ref, o_ref, xn_ref):
    @pl.when(pl.program_id(1) == 0)
    def _():
        xn_ref[...] = _rms(x_ref[...], g_ref[...]).astype(BF16)

    a = jnp.dot(xn_ref[...], wa_ref[...], preferred_element_type=F32) + ba_ref[...]
    b = jnp.dot(xn_ref[...], wb_ref[...], preferred_element_type=F32) + bb_ref[...]
    o_ref[...] = a * _sigmoid(b)


def rms_matmul_glu(x, g, w, b, tn=512, tm=1024):
    n, d = x.shape
    c = w.shape[1] // 2
    tm = _tile(n, tm)
    nj = c // tn
    b2 = b.reshape(1, 2 * c)
    return pl.pallas_call(
        _glu_kernel,
        grid=(n // tm, nj),
        in_specs=[pl.BlockSpec((tm, d), lambda i, j: (i, 0)),
                  pl.BlockSpec((1, d), lambda i, j: (0, 0)),
                  pl.BlockSpec((d, tn), lambda i, j: (0, j)),
                  pl.BlockSpec((d, tn), lambda i, j: (0, j + nj)),
                  pl.BlockSpec((1, tn), lambda i, j: (0, j)),
                  pl.BlockSpec((1, tn), lambda i, j: (0, j + nj))],
        out_specs=pl.BlockSpec((tm, tn), lambda i, j: (i, j)),
        out_shape=jax.ShapeDtypeStruct((n, c), F32),
        scratch_shapes=[pltpu.VMEM((tm, d), BF16)],
        compiler_params=_cparams("parallel", "arbitrary"),
        name="rms_matmul_glu",
    )(x, g.reshape(1, d), w, w, b2, b2)


def _ple_kernel(h_ref, y_ref, g_ref, wg_ref, bg_ref, p_ref, wp_ref, o_ref, xn_ref, hh_ref):
    j = pl.program_id(1)
    tn = o_ref.shape[1]

    @pl.when(j == 0)
    def _():
        hh = h_ref[...] + y_ref[...]
        xn_ref[...] = _rms(hh, g_ref[...]).astype(BF16)
        for jj in range(hh_ref.shape[0]):
            hh_ref[jj] = hh[:, jj * tn:(jj + 1) * tn]

    gate = _sigmoid(jnp.dot(xn_ref[...], wg_ref[...], preferred_element_type=F32) + bg_ref[...])
    proj = jnp.dot(p_ref[...].astype(BF16), wp_ref[...], preferred_element_type=F32)
    o_ref[...] = hh_ref[j] + proj * gate


def ple_mix(h, y, g, w_gate, b_gate, p, w_proj, tn=512, tm=512):
    n, d = h.shape
    pd = p.shape[1]
    tm = _tile(n, tm)
    return pl.pallas_call(
        _ple_kernel,
        grid=(n // tm, d // tn),
        in_specs=[pl.BlockSpec((tm, d), lambda i, j: (i, 0)),
                  pl.BlockSpec((tm, d), lambda i, j: (i, 0)),
                  pl.BlockSpec((1, d), lambda i, j: (0, 0)),
                  pl.BlockSpec((d, tn), lambda i, j: (0, j)),
                  pl.BlockSpec((1, tn), lambda i, j: (0, j)),
                  pl.BlockSpec((tm, pd), lambda i, j: (i, 0)),
                  pl.BlockSpec((pd, tn), lambda i, j: (0, j))],
        out_specs=pl.BlockSpec((tm, tn), lambda i, j: (i, j)),
        out_shape=jax.ShapeDtypeStruct((n, d), F32),
        scratch_shapes=[pltpu.VMEM((tm, d), BF16), pltpu.VMEM((d // tn, tm, tn), F32)],
        compiler_params=_cparams("parallel", "arbitrary"),
        name="ple_mix",
    )(h, y, g.reshape(1, d), w_gate, b_gate.reshape(1, d), p, w_proj)


def _rmsnorm_kernel(x_ref, g_ref, o_ref):
    o_ref[...] = _rms(x_ref[...], g_ref[...])


def rmsnorm(x, g, tm=512):
    n, d = x.shape
    tm = _tile(n, tm)
    return pl.pallas_call(
        _rmsnorm_kernel,
        grid=(n // tm,),
        in_specs=[pl.BlockSpec((tm, d), lambda i: (i, 0)),
                  pl.BlockSpec((1, d), lambda i: (0, 0))],
        out_specs=pl.BlockSpec((tm, d), lambda i: (i, 0)),
        out_shape=jax.ShapeDtypeStruct((n, d), F32),
        compiler_params=_cparams("parallel"),
        name="final_rmsnorm",
    )(x, g.reshape(1, d))


def _gdn_gates_kernel(x_ref, g_ref, wb_ref, wa_ref, alog_ref, dtb_ref, beta_ref, gc_ref, *, chunk):
    xn = _rms(x_ref[...], g_ref[...]).astype(BF16)
    b_raw = jnp.dot(xn, wb_ref[...], preferred_element_type=F32)
    a_raw = jnp.dot(xn, wa_ref[...], preferred_element_type=F32)
    beta_ref[...] = _sigmoid(b_raw)
    z = a_raw + dtb_ref[...]
    softplus = jnp.maximum(z, 0.0) + jnp.log(1.0 + jnp.exp(-jnp.abs(z)))
    gc = -jnp.exp(alog_ref[...]) * softplus
    pos = lax.broadcasted_iota(jnp.int32, gc.shape, 0) % chunk
    s = 1
    while s < chunk:
        gc = gc + jnp.where(pos >= s, pltpu.roll(gc, s, 0), 0.0)
        s *= 2
    gc_ref[...] = gc


def gdn_gates(x, g, w_b, w_a, a_log, dt_bias, chunk, tm=512):
    n, d = x.shape
    hv = w_b.shape[1]
    tm = _tile(n, tm)
    assert tm % chunk == 0
    return pl.pallas_call(
        functools.partial(_gdn_gates_kernel, chunk=chunk),
        grid=(n // tm,),
        in_specs=[pl.BlockSpec((tm, d), lambda i: (i, 0)),
                  pl.BlockSpec((1, d), lambda i: (0, 0)),
                  pl.BlockSpec((d, hv), lambda i: (0, 0)),
                  pl.BlockSpec((d, hv), lambda i: (0, 0)),
                  pl.BlockSpec((1, hv), lambda i: (0, 0)),
                  pl.BlockSpec((1, hv), lambda i: (0, 0))],
        out_specs=[pl.BlockSpec((tm, hv), lambda i: (i, 0)),
                   pl.BlockSpec((tm, hv), lambda i: (i, 0))],
        out_shape=[jax.ShapeDtypeStruct((n, hv), F32),
                   jax.ShapeDtypeStruct((n, hv), F32)],
        compiler_params=_cparams("parallel"),
        name="gdn_gates",
    )(x, g.reshape(1, d), w_b, w_a, a_log.reshape(1, hv), dt_bias.reshape(1, hv))


def _conv_silu(x, w_ref, prev):
    kconv = w_ref.shape[0]
    tpos = lax.broadcasted_iota(jnp.int32, x.shape, 0)
    y = x * w_ref[kconv - 1:kconv, :]
    for s in range(1, kconv):
        tail = 0.0 if prev is None else pltpu.roll(prev, s, 0)
        shifted = jnp.where(tpos >= s, pltpu.roll(x, s, 0), tail)
        y = y + shifted * w_ref[kconv - 1 - s:kconv - s, :]
    return _silu(y)


def _l2n(x):
    return x * lax.rsqrt(jnp.sum(x * x, axis=-1, keepdims=True) + EPS)


def _head_column(x, head):
    lane = lax.broadcasted_iota(jnp.int32, x.shape, 1)
    col = jnp.sum(jnp.where(lane == head, x, 0.0), axis=1, keepdims=True)
    return jnp.broadcast_to(col, (x.shape[0], GDN_HEAD))


def _gdn_chunk_pre(chains, c, between=None):
    def tick():
        if between is not None:
            between()

    row = lax.broadcasted_iota(jnp.int32, (c, c), 0)
    col = lax.broadcasted_iota(jnp.int32, (c, c), 1)
    eye = jnp.where(row == col, 1.0, 0.0)
    st = []
    for (q, k, v, bet, gcb) in chains:
        gi = gcb[:, :c]
        gj = gcb.T[:c, :]
        decay = jnp.exp(jnp.where(row >= col, gi - gj, -jnp.inf))
        egc = jnp.exp(gcb)
        kb = k * bet
        st.append(dict(q=q, k=k, gcb=gcb, decay=decay, egc=egc, kb=kb, vb=v * bet))
    for s in st:
        s["kq"] = _dot_nt(jnp.concatenate([s["kb"], s["q"]], axis=0), s["k"])
    tick()
    for s in st:
        s["attn"] = s["kq"][c:] * s["decay"]
        s["p"] = -jnp.where(row > col, s["kq"][:c] * s["decay"], 0.0)
        s["x"] = eye + s["p"]
    for s in st:
        s["p"] = _dot(s["p"], s["p"])
    tick()
    for _ in range(int(math.log2(c)) - 2):
        for s in st:
            s["m"] = _dot(jnp.concatenate([s["p"], s["x"]], axis=0), s["p"])
        tick()
        for s in st:
            s["p"] = s["m"][:c]
            s["x"] = s["x"] + s["m"][c:]
    for s in st:
        s["m"] = _dot(s["x"], s["p"])
    tick()
    for s in st:
        s["x"] = s["x"] + s["m"]
        s["rhs"] = jnp.concatenate([s["vb"], s["kb"] * s["egc"]], axis=1)
    for s in st:
        s["uw"] = _dot(s["x"], s["rhs"])
    tick()
    for s in st:
        s["glast"] = s["gcb"][c - 1:c, :]
        kdect = (s["k"] * jnp.exp(s["glast"] - s["gcb"])).T
        s["t"] = _dot(jnp.concatenate([kdect, s["attn"]], axis=0), s["uw"])
    tick()
    out = []
    hd = GDN_HEAD
    for s in st:
        t = s["t"]
        m = jnp.concatenate([-t[:hd, hd:], s["q"] * s["egc"] - t[hd:, hd:]], axis=0)
        out.append((m, t[:hd, :hd], t[hd:, :hd], jnp.exp(s["glast"])))
    return out


def _gdn_chunk_step(chains, c):
    rs = [_dot(m, s_prev) for (m, b, d, eg, s_prev) in chains]
    return [(r[GDN_HEAD:] + d, s_prev * eg + r[:GDN_HEAD] + b)
            for (m, b, d, eg, s_prev), r in zip(chains, rs)]


def _gdn_seq_kernel(q_ref, k_ref, v_ref, z_ref, beta_ref, gc_ref, cw_q_ref, cw_k_ref, cw_v_ref,
                    onorm_ref, o_ref, s_ref,
                    qs_ref, ks_ref, vs_ref, bs_ref, gs_ref, m_ref, b_ref, d_ref, eg_ref, st_ref,
                    *, t, c, group):
    hk = pl.program_id(1)
    n_chunks = t // c
    hd = GDN_HEAD
    qs_ref[...] = _l2n(_conv_silu(q_ref[0], cw_q_ref, None)) * (hd ** -0.5)
    ks_ref[...] = _l2n(_conv_silu(k_ref[0], cw_k_ref, None))
    vs_ref[...] = _conv_silu(v_ref[0], cw_v_ref, None)
    for i in range(2):
        bs_ref[i] = _head_column(beta_ref[0], 2 * hk + i)
        gs_ref[i] = _head_column(gc_ref[0], 2 * hk + i)
        st_ref[i] = jnp.zeros((hd, hd), F32)

    def rows(n, size):
        start = n * size
        return pl.ds(start if isinstance(start, int) else pl.multiple_of(start, size), size)

    def pre_group(it, between):
        loaded = []
        for g in range(group):
            n = it * group + g
            q = qs_ref[rows(n, c), :]
            k = ks_ref[rows(n, c), :]
            for i in range(2):
                loaded.append((n, i, q, k, vs_ref[rows(n, c), i * hd:(i + 1) * hd],
                               bs_ref[i, rows(n, c), :], gs_ref[i, rows(n, c), :]))
        pre_out = _gdn_chunk_pre([x[2:] for x in loaded], c, between)
        for (n, i, *_), (m, b, d, eg) in zip(loaded, pre_out):
            m_ref[i, rows(n, c + hd), :] = m.astype(BF16)
            b_ref[i, rows(n, hd), :] = b
            d_ref[i, rows(n, c), :] = d
            eg_ref[i, rows(n, SUBLANES), :] = jnp.broadcast_to(eg, (SUBLANES, hd))

    def step_chunk(n):
        chains = []
        for i in range(2):
            chains.append((m_ref[i, rows(n, c + hd), :], b_ref[i, rows(n, hd), :],
                           d_ref[i, rows(n, c), :], eg_ref[i, rows(n, SUBLANES), :][0:1],
                           st_ref[i]))
        for i, (o, s_new) in enumerate(_gdn_chunk_step(chains, c)):
            st_ref[i] = s_new
            d_ref[i, rows(n, c), :] = o

    n_groups = n_chunks // group
    pre_group(0, None)

    def body(it, carry):
        pending = list(range(group))

        def between():
            if pending:
                step_chunk((it - 1) * group + pending.pop(0))

        pre_group(it, between)
        while pending:
            step_chunk((it - 1) * group + pending.pop(0))
        return carry

    if n_groups <= 4:
        for it in range(1, n_groups):
            body(it, 0)
    else:
        lax.fori_loop(1, n_groups, body, 0)
    for g in range(group):
        step_chunk((n_groups - 1) * group + g)

    for i in range(2):
        s_ref[0, i] = st_ref[i]
        gated = _rms(d_ref[i], onorm_ref[...]) * _silu(z_ref[0, :, i * hd:(i + 1) * hd])
        o_ref[0, :, i * hd:(i + 1) * hd] = gated.astype(BF16)


def _gdn_step_kernel(q_ref, k_ref, v_ref, z_ref, beta_ref, gc_ref, cw_q_ref, cw_k_ref, cw_v_ref,
                     onorm_ref, pq_ref, pk_ref, pv_ref, s0_ref, o_ref, s_ref, *, c, bb, group):
    hk = pl.program_id(1)
    hd = GDN_HEAD

    def per_group(it, carry):
        loaded = []
        for g in range(group):
            bi = it * group + g
            loaded.append((bi, q_ref[bi], k_ref[bi], v_ref[bi], pq_ref[bi], pk_ref[bi], pv_ref[bi],
                           beta_ref[bi], gc_ref[bi], z_ref[bi], s0_ref[bi, 0], s0_ref[bi, 1]))
        chains, meta = [], []
        for (bi, xq, xk, xv, pq, pk, pv, beta, gc, z, s0a, s0b) in loaded:
            q = _l2n(_conv_silu(xq, cw_q_ref, pq)) * (hd ** -0.5)
            k = _l2n(_conv_silu(xk, cw_k_ref, pk))
            vv = _conv_silu(xv, cw_v_ref, pv)
            for i, s0 in enumerate((s0a, s0b)):
                chains.append((q, k, vv[:, i * hd:(i + 1) * hd],
                               _head_column(beta, 2 * hk + i), _head_column(gc, 2 * hk + i)))
                meta.append((bi, i, z[:, i * hd:(i + 1) * hd], s0))
        pre_out = _gdn_chunk_pre(chains, c)
        step_out = _gdn_chunk_step([p + (m[3],) for p, m in zip(pre_out, meta)], c)
        for (bi, i, z, _), (o, s_new) in zip(meta, step_out):
            s_ref[bi, i] = s_new
            gated = _rms(o, onorm_ref[...]) * _silu(z)
            o_ref[bi, :, i * hd:(i + 1) * hd] = gated.astype(BF16)
        return carry

    lax.fori_loop(0, bb // group, per_group, 0)


def gdn_delta(proj, beta, gc, conv_w, o_norm, conv_state, s0):
    b, t, _ = proj.shape
    hv = beta.shape[2]
    hq = hv // 2
    c = min(GDN_CHUNK, t)
    assert t % c == 0
    n_chunks = t // c
    kc = conv_w.shape[0]
    hd = GDN_HEAD
    q0, k0, v0, z0 = 0, hq, hq, 2 * hq
    bb = 1 if s0 is None else _tile(b, 16)

    in_specs = [pl.BlockSpec((bb, t, hd), lambda i, h: (i, 0, q0 + h)),
                pl.BlockSpec((bb, t, hd), lambda i, h: (i, 0, k0 + h)),
                pl.BlockSpec((bb, t, 2 * hd), lambda i, h: (i, 0, v0 + h)),
                pl.BlockSpec((bb, t, 2 * hd), lambda i, h: (i, 0, z0 + h)),
                pl.BlockSpec((bb, t, hv), lambda i, h: (i, 0, 0)),
                pl.BlockSpec((bb, t, hv), lambda i, h: (i, 0, 0)),
                pl.BlockSpec((kc, hd), lambda i, h: (0, q0 + h)),
                pl.BlockSpec((kc, hd), lambda i, h: (0, k0 + h)),
                pl.BlockSpec((kc, 2 * hd), lambda i, h: (0, v0 + h)),
                pl.BlockSpec((1, hd), lambda i, h: (0, 0))]
    args = [proj, proj, proj, proj, beta, gc, conv_w, conv_w, conv_w, o_norm.reshape(1, hd)]
    out_specs = [pl.BlockSpec((bb, t, 2 * hd), lambda i, h: (i, 0, h)),
                 pl.BlockSpec((bb, 2, hd, hd), lambda i, h: (i, h, 0, 0))]
    out_shape = [jax.ShapeDtypeStruct((b, t, hv * hd), BF16),
                 jax.ShapeDtypeStruct((b, hv, hd, hd), F32)]
    if s0 is None:
        group = 8 if n_chunks % 8 == 0 else (4 if n_chunks % 4 == 0 else 1)
        body = functools.partial(_gdn_seq_kernel, t=t, c=c, group=group)
        scratch = [pltpu.VMEM((t, hd), F32), pltpu.VMEM((t, hd), F32), pltpu.VMEM((t, 2 * hd), F32),
                   pltpu.VMEM((2, t, hd), F32), pltpu.VMEM((2, t, hd), F32),
                   pltpu.VMEM((2, n_chunks * (c + hd), hd), BF16),
                   pltpu.VMEM((2, n_chunks * hd, hd), F32),
                   pltpu.VMEM((2, t, hd), F32),
                   pltpu.VMEM((2, n_chunks * SUBLANES, hd), F32),
                   pltpu.VMEM((2, hd, hd), F32)]
    else:
        assert n_chunks == 1
        group = 16 if bb % 16 == 0 else (8 if bb % 8 == 0 else 1)
        body = functools.partial(_gdn_step_kernel, c=c, bb=bb, group=group)
        scratch = []
        in_specs += [pl.BlockSpec((bb, SUBLANES, hd), lambda i, h: (i, 0, q0 + h)),
                     pl.BlockSpec((bb, SUBLANES, hd), lambda i, h: (i, 0, k0 + h)),
                     pl.BlockSpec((bb, SUBLANES, 2 * hd), lambda i, h: (i, 0, v0 + h)),
                     pl.BlockSpec((bb, 2, hd, hd), lambda i, h: (i, h, 0, 0))]
        args += [conv_state, conv_state, conv_state, s0]
    return pl.pallas_call(
        body,
        grid=(b // bb, hq),
        in_specs=in_specs,
        out_specs=out_specs,
        out_shape=out_shape,
        scratch_shapes=scratch,
        compiler_params=_cparams("parallel", "parallel"),
        name="gdn_delta",
    )(*args)


def _conf_conv_kernel(x_ref, halo_ref, w_ref, b_ref, lg_ref, lb_ref, o_ref, xx_ref, *, bb, tm, kw, hpad):
    base = hpad - (kw - 1)
    span = tm + SUBLANES

    def per_batch(bi, carry):
        xx_ref[0:hpad, :] = halo_ref[bi]
        xx_ref[hpad:hpad + tm, :] = x_ref[bi]
        xx_ref[hpad + tm:hpad + span, :] = jnp.zeros((SUBLANES, xx_ref.shape[1]), F32)
        ch = xx_ref.shape[1]
        cblk = min(ch, 4 * LANES)
        parts = []
        for c0 in range(0, ch, cblk):
            acc = None
            for r in range(SUBLANES):
                z = None
                for j in range(kw):
                    if (base + j) % SUBLANES != r:
                        continue
                    a = base + j - r
                    term = xx_ref[a:a + span, c0:c0 + cblk] * w_ref[j:j + 1, c0:c0 + cblk]
                    z = term if z is None else z + term
                if z is None:
                    continue
                if r:
                    z = pltpu.roll(z, span - r, 0)
                acc = z[:tm] if acc is None else acc + z[:tm]
            parts.append(acc)
        acc = jnp.concatenate(parts, axis=1) + b_ref[...]
        mu = jnp.mean(acc, axis=-1, keepdims=True)
        xc = acc - mu
        y = xc * lax.rsqrt(jnp.mean(xc * xc, axis=-1, keepdims=True) + EPS)
        y = y * lg_ref[...] + lb_ref[...]
        o_ref[bi] = _silu(y).astype(BF16)
        return carry

    lax.fori_loop(0, bb, per_batch, 0)


def conf_conv(x, halo, w, b, ln_g, ln_b, bb):
    nb, tm, ch = x.shape
    hpad = halo.shape[1]
    kw = w.shape[0]
    assert nb % bb == 0 and hpad >= kw - 1
    return pl.pallas_call(
        functools.partial(_conf_conv_kernel, bb=bb, tm=tm, kw=kw, hpad=hpad),
        grid=(nb // bb,),
        in_specs=[pl.BlockSpec((bb, tm, ch), lambda i: (i, 0, 0)),
                  pl.BlockSpec((bb, hpad, ch), lambda i: (i, 0, 0)),
                  pl.BlockSpec((kw, ch), lambda i: (0, 0)),
                  pl.BlockSpec((1, ch), lambda i: (0, 0)),
                  pl.BlockSpec((1, ch), lambda i: (0, 0)),
                  pl.BlockSpec((1, ch), lambda i: (0, 0))],
        out_specs=pl.BlockSpec((bb, tm, ch), lambda i: (i, 0, 0)),
        out_shape=jax.ShapeDtypeStruct((nb, tm, ch), BF16),
        scratch_shapes=[pltpu.VMEM((hpad + tm + SUBLANES, ch), F32)],
        compiler_params=_cparams("parallel"),
        name="conf_conv",
    )(x, halo, w, b.reshape(1, ch), ln_g.reshape(1, ch), ln_b.reshape(1, ch))


def _extract_top(s, rounds):
    rows = s.shape[0]
    iota = lax.broadcasted_iota(jnp.int32, s.shape, 0)
    rank = jnp.full(s.shape, float(rounds), F32)
    vals = []
    for r in range(rounds):
        m = jnp.max(s, axis=0, keepdims=True)
        idx = jnp.min(jnp.where(s == m, iota, rows), axis=0, keepdims=True)
        sel = iota == idx
        rank = jnp.where(sel, float(r), rank)
        s = jnp.where(sel, -jnp.inf, s)
        vals.append(m)
    return rank, vals


def _peer_route_kernel(q_ref, k1_ref, k2_ref, r2_ref, e2_ref, lc_ref, co_ref, cand_ref):
    s1 = _dot_nt(k1_ref[...], q_ref[:, :PEER_HALF])
    s2 = _dot_nt(k2_ref[...], q_ref[:, PEER_HALF:])
    rank1, v1 = _extract_top(s1, PEER_TOPK)
    rank2, v2 = _extract_top(s2, PEER_TOPK)
    cand_ref[...] = jnp.full(cand_ref.shape, -jnp.inf, F32)
    for r, (a, b) in enumerate(_CANDS):
        cand_ref[r:r + 1, :] = v1[a] + v2[b]
    cand = cand_ref[...]
    crank, cvals = _extract_top(cand, PEER_TOPK)
    sel = jnp.where(crank < float(PEER_TOPK), 1.0, 0.0)
    zsum = jnp.sum(sel * jnp.exp(cand - cvals[0]), axis=0, keepdims=True)
    crow = lax.broadcasted_iota(jnp.int32, cand.shape, 0)
    lc = jnp.zeros_like(s1)
    pos = 0
    for a in range(PEER_TOPK):
        cnt = sum(1 for (aa, _) in _CANDS if aa == a)
        in_a = (crow >= pos) & (crow < pos + cnt)
        la = jnp.sum(jnp.where(in_a, sel, 0.0), axis=0, keepdims=True)
        lc = lc + jnp.where(rank1 == float(a), la, 0.0)
        pos += cnt
    r2_ref[0] = rank2
    e2_ref[0] = jnp.exp(s2 - v2[0])
    lc_ref[0] = lc
    co_ref[0] = jnp.exp(s1 - v1[0]) / zsum


def peer_route(q, keys1, keys2, tm=512):
    n, qd = q.shape
    tm = _tile(n, tm)
    nk = keys1.shape[0]
    assert qd == PEER_HEADS * 2 * PEER_HALF
    tab = jax.ShapeDtypeStruct((PEER_HEADS, nk, n), F32)
    tspec = pl.BlockSpec((1, nk, tm), lambda i, h: (h, 0, i))
    return pl.pallas_call(
        _peer_route_kernel,
        grid=(n // tm, PEER_HEADS),
        in_specs=[pl.BlockSpec((tm, 2 * PEER_HALF), lambda i, h: (i, h)),
                  pl.BlockSpec(keys1.shape, lambda i, h: (0, 0)),
                  pl.BlockSpec(keys2.shape, lambda i, h: (0, 0))],
        out_specs=[tspec, tspec, tspec, tspec],
        out_shape=[tab, tab, tab, tab],
        scratch_shapes=[pltpu.VMEM((_NCAND_PAD, tm), F32)],
        compiler_params=_cparams("parallel", "parallel"),
        name="peer_route",
    )(q, keys1, keys2)


def _peer_mix_kernel(xn_ref, u_ref, vt_ref, r2_ref, e2_ref, lc_ref, co_ref, o_ref,
                     acc_ref, a_ref, *, eb, sub):
    j = pl.program_id(1)
    nk = PEER_NKEYS
    nsub = eb // sub
    tm = xn_ref.shape[0]
    ct = LANES

    @pl.when(j == 0)
    def _():
        acc_ref[...] = jnp.zeros_like(acc_ref)

    def hidden(s):
        return _dot_nt(u_ref[s * sub:(s + 1) * sub, :], xn_ref[...])

    def gated(s, ht):
        for ii in range(sub // nk):
            i1 = j * (eb // nk) + s * (sub // nk) + ii
            lrows = [lc_ref[h, pl.ds(i1, 1), :] for h in range(PEER_HEADS)]
            crows = [co_ref[h, pl.ds(i1, 1), :] for h in range(PEER_HEADS)]
            for c0 in range(0, tm, ct):
                hv = ht[ii * nk:(ii + 1) * nk, c0:c0 + ct]
                act = 0.5 * hv * (1.0 + lax.erf(hv * (2.0 ** -0.5)))
                gate = None
                for h in range(PEER_HEADS):
                    term = jnp.where(r2_ref[h, :, c0:c0 + ct] < lrows[h][:, c0:c0 + ct],
                                     e2_ref[h, :, c0:c0 + ct] * crows[h][:, c0:c0 + ct], 0.0)
                    gate = term if gate is None else gate + term
                a_ref[s, ii * nk:(ii + 1) * nk, c0:c0 + ct] = (gate * act).astype(BF16)

    hts = [hidden(s) for s in range(nsub)]
    for s in range(nsub):
        gated(s, hts[s])
        acc_ref[...] += jnp.dot(vt_ref[:, s * sub:(s + 1) * sub], a_ref[s], preferred_element_type=F32)

    @pl.when(j == pl.num_programs(1) - 1)
    def _():
        o_ref[...] = acc_ref[...].T


def peer_mix(xn, u, vt, layer, r2, e2, lc, co, tm=512, eb=1024, sub=512):
    n, d = xn.shape
    ne = u.shape[1]
    tm = _tile(n, tm)
    tspec = pl.BlockSpec((PEER_HEADS, PEER_NKEYS, tm), lambda i, j: (0, 0, i))
    return pl.pallas_call(
        functools.partial(_peer_mix_kernel, eb=eb, sub=sub),
        grid=(n // tm, ne // eb),
        in_specs=[pl.BlockSpec((tm, d), lambda i, j: (i, 0)),
                  pl.BlockSpec((None, eb, d), lambda i, j: (layer, j, 0)),
                  pl.BlockSpec((None, d, eb), lambda i, j: (layer, 0, j)),
                  tspec, tspec, tspec, tspec],
        out_specs=pl.BlockSpec((tm, d), lambda i, j: (i, 0)),
        out_shape=jax.ShapeDtypeStruct((n, d), F32),
        scratch_shapes=[pltpu.VMEM((d, tm), F32), pltpu.VMEM((eb // sub, sub, tm), BF16)],
        compiler_params=_cparams("parallel", "arbitrary"),
        name="peer_mix",
    )(xn, u, vt, r2, e2, lc, co)


def _cast_transpose_kernel(x_ref, o_ref):
    o_ref[0] = x_ref[0].T.astype(BF16)


def cast_transpose(x, tr=512):
    nl, r, c = x.shape
    tr = _tile(r, tr)
    return pl.pallas_call(
        _cast_transpose_kernel,
        grid=(nl, r // tr),
        in_specs=[pl.BlockSpec((1, tr, c), lambda l, i: (l, i, 0))],
        out_specs=pl.BlockSpec((1, c, tr), lambda l, i: (l, 0, i)),
        out_shape=jax.ShapeDtypeStruct((nl, c, r), BF16),
        compiler_params=_cparams("parallel", "parallel"),
        name="cast_transpose",
    )(x)


def _prep_weights(norm_mix, norm_ffn, norm_ple, final_norm,
                  gdn_w_in, gdn_conv_w, gdn_a_log, gdn_dt_bias, gdn_o_norm, gdn_w_out,
                  conf_w_in, conf_b_in, conf_dw_w, conf_dw_b, conf_ln_g, conf_ln_b, conf_w_out, conf_b_out,
                  peer_w_q, peer_keys1, peer_keys2, peer_u, peer_v,
                  ple_w_proj, ple_w_gate, ple_b_gate):
    hv = gdn_a_log.shape[1]
    main = gdn_w_in.shape[2] - 2 * hv
    return dict(
        gdn_w_main=gdn_w_in[:, :, :main].astype(BF16),
        gdn_w_b=gdn_w_in[:, :, main:main + hv].astype(BF16),
        gdn_w_a=gdn_w_in[:, :, main + hv:].astype(BF16),
        gdn_w_out=gdn_w_out.astype(BF16),
        conf_w_in=conf_w_in.astype(BF16),
        conf_w_out=conf_w_out.astype(BF16),
        peer_w_q=peer_w_q.astype(BF16),
        peer_keys1=peer_keys1.astype(BF16),
        peer_keys2=peer_keys2.astype(BF16),
        peer_u=peer_u.astype(BF16),
        peer_vt=cast_transpose(peer_v),
        ple_w_proj=ple_w_proj.astype(BF16),
        ple_w_gate=ple_w_gate.astype(BF16),
    )


def _trunk(x, p, gdn_s, gdn_buf, conf_buf, wts, pw):
    (norm_mix, norm_ffn, norm_ple, final_norm,
     gdn_w_in, gdn_conv_w, gdn_a_log, gdn_dt_bias, gdn_o_norm, gdn_w_out,
     conf_w_in, conf_b_in, conf_dw_w, conf_dw_b, conf_ln_g, conf_ln_b, conf_w_out, conf_b_out,
     peer_w_q, peer_keys1, peer_keys2, peer_u, peer_v,
     ple_w_proj, ple_w_gate, ple_b_gate) = wts
    b, t, d = x.shape
    n = b * t
    depth = norm_mix.shape[0]
    has_state = gdn_s is not None
    h = x.reshape(n, d)
    new_s, new_gbuf, new_cbuf = [], [], []
    for i in range(depth):
        j = i // 2
        if i % 2 == 0:
            hv = gdn_a_log.shape[1]
            conv_dim = gdn_conv_w.shape[2]
            kc = gdn_conv_w.shape[1]
            main = pw["gdn_w_main"].shape[2]
            proj = rms_matmul(h, norm_mix[i], pw["gdn_w_main"][j], main, tn=1024)
            chunk = min(GDN_CHUNK, t)
            beta, gc = gdn_gates(h, norm_mix[i], pw["gdn_w_b"][j], pw["gdn_w_a"][j],
                                 gdn_a_log[j], gdn_dt_bias[j], chunk)
            proj3 = proj.reshape(b, t, main)
            if has_state:
                cstate = jnp.pad(gdn_buf[j], ((0, 0), (SUBLANES - (kc - 1), 0), (0, 0)))
                s0 = gdn_s[j]
            else:
                cstate, s0 = None, None
            o, s_fin = gdn_delta(proj3, beta.reshape(b, t, hv), gc.reshape(b, t, hv), gdn_conv_w[j],
                                 gdn_o_norm[j], cstate, s0)
            new_s.append(s_fin)
            new_gbuf.append(proj3[:, t - (kc - 1):, :conv_dim])
            h = matmul_bias_res(o.reshape(n, hv * GDN_HEAD), pw["gdn_w_out"][j],
                                jnp.zeros((d,), F32), h)
        else:
            kw = conf_dw_w.shape[1]
            ch = conf_dw_w.shape[2]
            glu = rms_matmul_glu(h, norm_mix[i], pw["conf_w_in"][j], conf_b_in[j])
            glu3 = glu.reshape(b, t, ch)
            hpad = 32
            if has_state:
                prev = conf_buf[j]
                halo = jnp.pad(prev, ((0, 0), (hpad - (kw - 1), 0), (0, 0)))
                c = conf_conv(glu3, halo, conf_dw_w[j], conf_dw_b[j], conf_ln_g[j], conf_ln_b[j], bb=8)
                new_cbuf.append(jnp.concatenate([prev, glu3], axis=1)[:, t:])
            else:
                tmc = _tile(t, 256)
                nt = t // tmc
                tiles = glu3.reshape(b, nt, tmc, ch)
                halo = jnp.concatenate([jnp.zeros((b, 1, hpad, ch), F32), tiles[:, :-1, tmc - hpad:]], axis=1)
                c = conf_conv(tiles.reshape(b * nt, tmc, ch), halo.reshape(b * nt, hpad, ch),
                              conf_dw_w[j], conf_dw_b[j], conf_ln_g[j], conf_ln_b[j], bb=1)
                new_cbuf.append(glu3[:, t - (kw - 1):])
            h = matmul_bias_res(c.reshape(n, ch), pw["conf_w_out"][j], conf_b_out[j], h)
        q, xn = rms_matmul_xn(h, norm_ffn[i], pw["peer_w_q"][i], tn=1024)
        r2, e2, lc, co = peer_route(q, pw["peer_keys1"][i], pw["peer_keys2"][i])
        y = peer_mix(xn, pw["peer_u"], pw["peer_vt"], i, r2, e2, lc, co)
        h = ple_mix(h, y, norm_ple[i], pw["ple_w_gate"][i], ple_b_gate[i],
                    p[i].reshape(n, -1), pw["ple_w_proj"][i])
    out = rmsnorm(h, final_norm).reshape(b, t, d)
    return out, jnp.stack(new_s), jnp.stack(new_gbuf), jnp.stack(new_cbuf)


def kernel(x_prompt, x_sample, p_prompt, p_sample, state_gdn_recurrent, state_gdn_conv, state_conf_conv, norm_mix, norm_ffn, norm_ple, final_norm, gdn_w_in, gdn_conv_w, gdn_a_log, gdn_dt_bias, gdn_o_norm, gdn_w_out, conf_w_in, conf_b_in, conf_dw_w, conf_dw_b, conf_ln_g, conf_ln_b, conf_w_out, conf_b_out, peer_w_q, peer_keys1, peer_keys2, peer_u, peer_v, ple_w_proj, ple_w_gate, ple_b_gate):
    wts = (norm_mix, norm_ffn, norm_ple, final_norm,
           gdn_w_in, gdn_conv_w, gdn_a_log, gdn_dt_bias, gdn_o_norm, gdn_w_out,
           conf_w_in, conf_b_in, conf_dw_w, conf_dw_b, conf_ln_g, conf_ln_b, conf_w_out, conf_b_out,
           peer_w_q, peer_keys1, peer_keys2, peer_u, peer_v,
           ple_w_proj, ple_w_gate, ple_b_gate)
    pw = _prep_weights(*wts)
    y_p, s_p, gb_p, cb_p = _trunk(x_prompt, p_prompt, None, None, None, wts, pw)
    y_s, s_s, gb_s, cb_s = _trunk(x_sample, p_sample, state_gdn_recurrent, state_gdn_conv,
                                  state_conf_conv, wts, pw)
    return (y_p, y_s, s_p, s_s, gb_p, gb_s, cb_p, cb_s)
```
